```python
import math
import jax, jax.numpy as jnp
from jax import lax
import numpy as np

D_MODEL = 2048
BATCH = 4
SEQ = 4096
DEPTH = 2

CHUNK = 64
HEAD_DIM = 128
N_HEADS_A = D_MODEL // (2 * HEAD_DIM)
N_HEADS_B = D_MODEL // (2 * HEAD_DIM)
D_A = N_HEADS_A * HEAD_DIM
D_B = N_HEADS_B * HEAD_DIM
N_IDX_HEADS = 16
IDX_DIM = 64
TOPK_MAX = 256
ROPE_THETA = 10000.0
SPARSE_Q_BLOCK = 64
SB_Q_BLOCK = 128
D_FF = 5632
N_EXPERTS = 8
TOP_K_EXPERTS = 2
D_EXPERT = 7168
LN_EPS = 1e-5
DN_ALPHA = (2.0 * DEPTH) ** 0.25
DN_BETA = (8.0 * DEPTH) ** -0.25
N_DENSE = (DEPTH + 1) // 2
N_MOE = DEPTH // 2

_SPLIT_SIZES = (D_A, D_A, D_A,
                N_IDX_HEADS * IDX_DIM, IDX_DIM, N_IDX_HEADS,
                D_B, D_B, D_B,
                D_MODEL, D_MODEL)
IN_COLS = sum(_SPLIT_SIZES)
_OFFSETS = tuple(int(o) for o in np.cumsum((0,) + _SPLIT_SIZES))
SPLIT_POINTS = _OFFSETS[1:-1]
_VALUE_GROUPS = (2, 8)

kernel_name = 'hybrid_dsa_stickbreak_moe_deepnorm'

F32 = jnp.float32


def layer_norm(x, g, b):
    xf = x.astype(F32)
    mu = jnp.mean(xf, axis=-1, keepdims=True)
    var = jnp.mean(jnp.square(xf - mu), axis=-1, keepdims=True)
    y = (xf - mu) * lax.rsqrt(var + LN_EPS) * g.astype(F32) + b.astype(F32)
    return y.astype(x.dtype)


def rope_tables(positions, dim):
    inv_freq = ROPE_THETA ** (-jnp.arange(0, dim, 2, dtype=F32) / dim)
    ang = positions.astype(F32)[..., None] * inv_freq
    return jnp.cos(ang), jnp.sin(ang)


def apply_rope(x, cos, sin):
    xf = x.astype(F32)
    x1, x2 = jnp.split(xf, 2, axis=-1)
    c = cos[:, :, None, :]
    s = sin[:, :, None, :]
    return jnp.concatenate([x1 * c - x2 * s, x2 * c + x1 * s], axis=-1).astype(x.dtype)


def to_blocks(a, blk):
    b, s = a.shape[:2]
    return jnp.moveaxis(a.reshape((b, s // blk, blk) + a.shape[2:]), 1, 0)


def from_blocks(a):
    a = jnp.moveaxis(a, 0, 1)
    return a.reshape((a.shape[0], a.shape[1] * a.shape[2]) + a.shape[3:])


def dsa_sparse_attention(q, k, v, q_idx, k_idx, w_idx):
    bsz, s_len, n_h, dh = q.shape
    n_sel = min(TOPK_MAX, s_len // 4)
    key_chunk = jnp.arange(s_len) // CHUNK
    k_flat = k.reshape(bsz, s_len, n_h * dh)
    v_flat = v.reshape(bsz, s_len, n_h * dh)
    gather = jax.vmap(lambda table, idx: table[idx])
    scale = 1.0 / math.sqrt(dh)
    k_idx_f = k_idx.astype(F32)

    def block(args):
        qb, qib, wb, start = args
        t = start + jnp.arange(SPARSE_Q_BLOCK)
        q_chunk = t // CHUNK
        admissible = key_chunk[None, :] <= q_chunk[:, None]
        logits = jnp.einsum('bqhd,bsd->bqhs', qib.astype(F32), k_idx_f)
        score = jnp.einsum('bqh,bqhs->bqs', wb.astype(F32), jax.nn.relu(logits))
        score = jnp.where(admissible[None], score, -jnp.inf)
        _, sel = lax.top_k(score, n_sel)
        valid = (sel // CHUNK) <= q_chunk[None, :, None]
        kg = gather(k_flat, sel).reshape(bsz, SPARSE_Q_BLOCK, n_sel, n_h, dh)
        vg = gather(v_flat, sel).reshape(bsz, SPARSE_Q_BLOCK, n_sel, n_h, dh)
        s = jnp.einsum('bqhd,bqnhd->bqhn', qb.astype(F32), kg.astype(F32)) * scale
        s = jnp.where(valid[:, :, None, :], s, -jnp.inf)
        p = jax.nn.softmax(s, axis=-1)
        o = jnp.einsum('bqhn,bqnhd->bqhd', p, vg.astype(F32))
        return o.astype(q.dtype)

    starts = jnp.arange(s_len // SPARSE_Q_BLOCK) * SPARSE_Q_BLOCK
    out = lax.map(block, (to_blocks(q, SPARSE_Q_BLOCK), to_blocks(q_idx, SPARSE_Q_BLOCK),
                          to_blocks(w_idx, SPARSE_Q_BLOCK), starts))
    return from_blocks(out)


def stick_breaking_attention(q, k, v):
    bsz, s_len, n_h, dh = q.shape
    key_pos = jnp.arange(s_len)
    scale = 1.0 / math.sqrt(dh)
    k_f = k.astype(F32)
    v_f = v.astype(F32)

    def block(args):
        qb, start = args
        t = start + jnp.arange(SB_Q_BLOCK)
        strict = (key_pos[None, :] < t[:, None])[None, None]
        z = jnp.einsum('bqhd,bshd->bhqs', qb.astype(F32), k_f) * scale
        log_keep = jnp.where(strict, jax.nn.log_sigmoid(-z), 0.0)
        later = lax.cumsum(log_keep, axis=3, reverse=True) - log_keep
        a = jnp.where(strict, jnp.exp(jax.nn.log_sigmoid(z) + later), 0.0)
        o = jnp.einsum('bhqs,bshd->bqhd', a, v_f)
        return o.astype(q.dtype)

    starts = jnp.arange(s_len // SB_Q_BLOCK) * SB_Q_BLOCK
    out = lax.map(block, (to_blocks(q, SB_Q_BLOCK), starts))
    return from_blocks(out)


def hybrid_mixer(x, cos_h, sin_h, cos_i, sin_i, w_in, w_proj_a, w_proj_b, w_out):
    bsz, s_len, _ = x.shape
    h = jnp.einsum('bsd,dc->bsc', x, w_in)
    qa, ka, va, qi, ki, wi, qb, kb, vb, ga, gb = jnp.split(h, SPLIT_POINTS, axis=-1)
    heads = lambda t, n, d: t.reshape(bsz, s_len, n, d)
    qa = apply_rope(heads(qa, N_HEADS_A, HEAD_DIM), cos_h, sin_h)
    ka = apply_rope(heads(ka, N_HEADS_A, HEAD_DIM), cos_h, sin_h)
    va = heads(va, N_HEADS_A, HEAD_DIM)
    qi = apply_rope(heads(qi, N_IDX_HEADS, IDX_DIM), cos_i, sin_i)
    ki = apply_rope(ki[:, :, None, :], cos_i, sin_i)[:, :, 0, :]
    o_a = dsa_sparse_attention(qa, ka, va, qi, ki, wi).reshape(bsz, s_len, D_A)
    o_b = stick_breaking_attention(heads(qb, N_HEADS_B, HEAD_DIM), heads(kb, N_HEADS_B, HEAD_DIM),
                                   heads(vb, N_HEADS_B, HEAD_DIM)).reshape(bsz, s_len, D_B)
    merged = (jax.nn.sigmoid(ga) * jnp.einsum('bsc,cd->bsd', o_a, w_proj_a)
              + jax.nn.sigmoid(gb) * jnp.einsum('bsc,cd->bsd', o_b, w_proj_b))
    return jnp.einsum('bsd,de->bse', merged, w_out)


def swiglu(x, w_gate, w_up, w_down):
    return (jax.nn.silu(x @ w_gate) * (x @ w_up)) @ w_down


def moe_swiglu(x, router_w, w_gate, w_up, w_down):
    bsz, s_len, d = x.shape
    xt = x.reshape(bsz * s_len, d)
    logits = (xt @ router_w).astype(F32)
    top_val, top_idx = lax.top_k(logits, TOP_K_EXPERTS)
    gates = jax.nn.softmax(top_val, axis=-1)
    comb = jnp.sum(jax.nn.one_hot(top_idx, N_EXPERTS, dtype=F32) * gates[..., None], axis=1)
    y = jnp.zeros((bsz * s_len, d), F32)
    for e in range(N_EXPERTS):
        y = y + comb[:, e:e + 1] * swiglu(xt, w_gate[e], w_up[e], w_down[e]).astype(F32)
    return y.astype(x.dtype).reshape(bsz, s_len, d)


def setup_inputs(seed: int = 0) -> dict:
    key = jax.random.key(seed)
    ks = jax.random.split(key, 24)

    def normal(k, shape, fan_in, scale=1.0):
        return jax.random.normal(k, shape, F32) * (scale * fan_in ** -0.5)

    def gain(k, shape):
        return 1.0 + 0.02 * jax.random.normal(k, shape, F32)

    def bias(k, shape):
        return 0.02 * jax.random.normal(k, shape, F32)

    x = jax.random.normal(ks[0], (BATCH, SEQ, D_MODEL), F32)
    offsets = jax.random.randint(ks[1], (BATCH, 1), 0, 4096, dtype=jnp.int32)
    positions = offsets + jnp.arange(SEQ, dtype=jnp.int32)[None, :]
    col_scale = np.ones((IN_COLS,), np.float32)
    for g in _VALUE_GROUPS:
        col_scale[_OFFSETS[g]:_OFFSETS[g + 1]] = DN_BETA
    w_in = normal(ks[2], (DEPTH, D_MODEL, IN_COLS), D_MODEL) * jnp.asarray(col_scale)
    return {
        'x': x,
        'positions': positions,
        'ln_in_g': gain(ks[3], (D_MODEL,)),
        'ln_in_b': bias(ks[4], (D_MODEL,)),
        'w_in': w_in,
        'w_proj_a': normal(ks[5], (DEPTH, D_A, D_MODEL), D_A, DN_BETA),
        'w_proj_b': normal(ks[6], (DEPTH, D_B, D_MODEL), D_B, DN_BETA),
        'w_out': normal(ks[7], (DEPTH, D_MODEL, D_MODEL), D_MODEL, DN_BETA),
        'ln_mix_g': gain(ks[8], (DEPTH, D_MODEL)),
        'ln_mix_b': bias(ks[9], (DEPTH, D_MODEL)),
        'ffn_w_gate': normal(ks[10], (N_DENSE, D_MODEL, D_FF), D_MODEL),
        'ffn_w_up': normal(ks[11], (N_DENSE, D_MODEL, D_FF), D_MODEL),
        'ffn_w_down': normal(ks[12], (N_DENSE, D_FF, D_MODEL), D_FF, DN_BETA),
        'router_w': normal(ks[13], (N_MOE, D_MODEL, N_EXPERTS), D_MODEL),
        'moe_w_gate': normal(ks[14], (N_MOE, N_EXPERTS, D_MODEL, D_EXPERT), D_MODEL),
        'moe_w_up': normal(ks[15], (N_MOE, N_EXPERTS, D_MODEL, D_EXPERT), D_MODEL),
        'moe_w_down': normal(ks[16], (N_MOE, N_EXPERTS, D_EXPERT, D_MODEL), D_EXPERT, DN_BETA),
        'ln_ffn_g': gain(ks[17], (DEPTH, D_MODEL)),
        'ln_ffn_b': bias(ks[18], (DEPTH, D_MODEL)),
    }


def reference(x, positions, ln_in_g, ln_in_b, w_in, w_proj_a, w_proj_b, w_out, ln_mix_g, ln_mix_b,
              ffn_w_gate, ffn_w_up, ffn_w_down, router_w, moe_w_gate, moe_w_up, moe_w_down,
              ln_ffn_g, ln_ffn_b):
    cos_h, sin_h = rope_tables(positions, HEAD_DIM)
    cos_i, sin_i = rope_tables(positions, IDX_DIM)
    x = layer_norm(x, ln_in_g, ln_in_b)
    for layer in range(DEPTH):
        mix = hybrid_mixer(x, cos_h, sin_h, cos_i, sin_i, w_in[layer], w_proj_a[layer],
                           w_proj_b[layer], w_out[layer])
        x = layer_norm(DN_ALPHA * x + mix, ln_mix_g[layer], ln_mix_b[layer])
        i = layer // 2
        if layer % 2 == 0:
            f = swiglu(x, ffn_w_gate[i], ffn_w_up[i], ffn_w_down[i])
        else:
            f = moe_swiglu(x, router_w[i], moe_w_gate[i], moe_w_up[i], moe_w_down[i])
        x = layer_norm(DN_ALPHA * x + f, ln_ffn_g[layer], ln_ffn_b[layer])
    return x
```

```python
import functools
import math

import jax
import jax.numpy as jnp
from jax import lax
from jax.experimental import pallas as pl
from jax.experimental.pallas import tpu as pltpu

F32 = jnp.float32
BF16 = jnp.bfloat16
I32 = jnp.int32

LN_EPS = 1e-5
HEAD_DIM = 128
IDX_HEADS = 16
IDX_DIM = 64
CHUNK = 64
TOPK_MAX = 256
ROPE_THETA = 10000.0
TOP_K_EXPERTS = 2
LANES = 128
VMEM_LIMIT = 56 * 1024 * 1024
MASK_NEG = -1e30
KEY_NEG_INF = -2139095041
INT_MIN = -2147483648


def _params(sem):
    return pltpu.CompilerParams(dimension_semantics=sem, vmem_limit_bytes=VMEM_LIMIT)


def _pick(n, pref):
    b = min(n, pref)
    while n % b:
        b //= 2
    return b


def _layer_norm(v, g, b):
    mu = jnp.mean(v, axis=-1, keepdims=True)
    c = v - mu
    var = jnp.mean(c * c, axis=-1, keepdims=True)
    return c * lax.rsqrt(var + LN_EPS) * g + b


def _ln_kernel(x_ref, g_ref, b_ref, of_ref, ob_ref):
    y = _layer_norm(x_ref[...], g_ref[...], b_ref[...])
    of_ref[...] = y
    ob_ref[...] = y.astype(BF16)


def _input_ln(x2d, g, b):
    n, d = x2d.shape
    bm = _pick(n, 512)
    return pl.pallas_call(
        _ln_kernel,
        grid=(n // bm,),
        in_specs=[pl.BlockSpec((bm, d), lambda i: (i, 0)),
                  pl.BlockSpec((1, d), lambda i: (0, 0)),
                  pl.BlockSpec((1, d), lambda i: (0, 0))],
        out_specs=[pl.BlockSpec((bm, d), lambda i: (i, 0)),
                   pl.BlockSpec((bm, d), lambda i: (i, 0))],
        out_shape=[jax.ShapeDtypeStruct((n, d), F32), jax.ShapeDtypeStruct((n, d), BF16)],
        compiler_params=_params(("parallel",)),
        name="input_ln",
    )(x2d, g.reshape(1, d), b.reshape(1, d))


def _inproj_kernel(x_ref, w_ref, c128_ref, s128_ref, c64_ref, sa64_ref, sb64_ref, o_ref, *,
                   rope128_tiles, rope64_tiles):
    j = pl.program_id(1)
    y = jnp.dot(x_ref[...], w_ref[...], preferred_element_type=F32)
    n_heads = y.shape[1] // LANES
    lo64, hi64 = rope64_tiles
    is128 = j < rope128_tiles
    is64 = jnp.logical_and(j >= lo64, j < hi64)

    @pl.when(is128)
    def _():
        c = c128_ref[...]
        s = s128_ref[...]
        for h in range(n_heads):
            yh = y[:, h * LANES:(h + 1) * LANES]
            o_ref[:, h * LANES:(h + 1) * LANES] = (
                yh * c + pltpu.roll(yh, HEAD_DIM // 2, 1) * s).astype(o_ref.dtype)

    @pl.when(is64)
    def _():
        c = c64_ref[...]
        sa = sa64_ref[...]
        sb = sb64_ref[...]
        for h in range(n_heads):
            yh = y[:, h * LANES:(h + 1) * LANES]
            o_ref[:, h * LANES:(h + 1) * LANES] = (
                yh * c + pltpu.roll(yh, LANES - IDX_DIM // 2, 1) * sa
                + pltpu.roll(yh, IDX_DIM // 2, 1) * sb).astype(o_ref.dtype)

    @pl.when(jnp.logical_not(jnp.logical_or(is128, is64)))
    def _():
        o_ref[...] = y.astype(o_ref.dtype)


def _inproj(xb, w, tabs, *, bn, rope128_tiles, rope64_tiles, out_dtype):
    n, d = xb.shape
    cols = w.shape[1]
    bm = _pick(n, 1024)
    tab_spec = pl.BlockSpec((bm, LANES), lambda i, j: (i, 0))
    return pl.pallas_call(
        functools.partial(_inproj_kernel, rope128_tiles=rope128_tiles, rope64_tiles=rope64_tiles),
        grid=(n // bm, cols // bn),
        in_specs=[pl.BlockSpec((bm, d), lambda i, j: (i, 0)),
                  pl.BlockSpec((d, bn), lambda i, j: (0, j))] + [tab_spec] * 5,
        out_specs=pl.BlockSpec((bm, bn), lambda i, j: (i, j)),
        out_shape=jax.ShapeDtypeStruct((n, cols), out_dtype),
        compiler_params=_params(("parallel", "arbitrary")),
        name="inproj",
    )(xb, w, *tabs)


def _indexer_kernel(qit_ref, ki_ref, wit_ref, bias_ref, skey_ref, *, tq, n_sel):
    qb = pl.program_id(1)
    n_kt = bias_ref.shape[0] // tq
    q_chunk = (qb * tq + lax.broadcasted_iota(I32, (tq, tq), 1)) // CHUNK
    k_local = lax.broadcasted_iota(I32, (tq, tq), 0)

    def score_tile(kt, carry):
        k0 = pl.multiple_of(kt * tq, tq)
        k_t = ki_ref[pl.ds(k0, tq), :]
        acc = jnp.zeros((tq, tq), F32)
        for h in range(IDX_HEADS):
            lg = jnp.dot(k_t, qit_ref[h], preferred_element_type=F32)
            acc = acc + jnp.maximum(lg, 0.0) * wit_ref[h:h + 1, :]
        acc = jnp.where((k0 + k_local) // CHUNK <= q_chunk, acc, -jnp.inf)
        bits = pltpu.bitcast(acc, I32)
        skey_ref[pl.ds(k0, tq), :] = bits ^ ((bits >> 31) & 0x7FFFFFFF)
        return carry

    lax.fori_loop(0, qb + 1, score_tile, 0)

    def count_ge(cand):
        def body(kt, acc8):
            k0 = pl.multiple_of(kt * tq, tq)
            m = jnp.where(skey_ref[pl.ds(k0, tq), :] >= cand, 1, 0).astype(I32)
            return acc8 + m.reshape(tq // 8, 8, tq).sum(axis=0)
        acc8 = lax.fori_loop(0, qb + 1, body, jnp.zeros((8, tq), I32))
        return acc8.sum(axis=0, keepdims=True)

    def bit_step(i, ans):
        cand = ans + lax.shift_left(jnp.int32(1), 31 - i)
        return jnp.where(count_ge(cand) >= n_sel, cand, ans)

    n_bits = jnp.where((qb + 1) * tq > n_sel, 32, 0)
    thr = lax.fori_loop(0, n_bits, bit_step, jnp.full((1, tq), INT_MIN, I32))
    thr = jnp.maximum(thr, KEY_NEG_INF + 1)

    def bias_tile(kt, carry):
        k0 = pl.multiple_of(kt * tq, tq)
        sel = skey_ref[pl.ds(k0, tq), :] >= thr
        bias_ref[pl.ds(k0, tq), :] = jnp.where(sel, 0.0, MASK_NEG).astype(BF16)
        return carry

    lax.fori_loop(0, qb + 1, bias_tile, 0)

    def fill_tile(kt, carry):
        k0 = pl.multiple_of(kt * tq, tq)
        bias_ref[pl.ds(k0, tq), :] = jnp.full((tq, tq), MASK_NEG, BF16)
        return carry

    lax.fori_loop(qb + 1, n_kt, fill_tile, 0)


def _indexer(qit, ki, wit, *, tq, n_sel):
    bsz, _, _, s_len = qit.shape
    return pl.pallas_call(
        functools.partial(_indexer_kernel, tq=tq, n_sel=n_sel),
        grid=(bsz, s_len // tq),
        in_specs=[pl.BlockSpec((None, IDX_HEADS, IDX_DIM, tq), lambda b, q: (b, 0, 0, q)),
                  pl.BlockSpec((None, s_len, IDX_DIM), lambda b, q: (b, 0, 0)),
                  pl.BlockSpec((None, IDX_HEADS, tq), lambda b, q: (b, 0, q))],
        out_specs=pl.BlockSpec((None, s_len, tq), lambda b, q: (b, 0, q)),
        out_shape=jax.ShapeDtypeStruct((bsz, s_len, s_len), BF16),
        scratch_shapes=[pltpu.VMEM((s_len, tq), I32)],
        compiler_params=_params(("parallel", "arbitrary")),
        name="indexer_topk",
    )(qit, ki, wit)


def _sparse_attn_kernel(qt_ref, k_ref, vt_ref, bias_ref, ot_ref, *, tq, n_heads):
    qb = pl.program_id(1)
    scale = 1.0 / math.sqrt(HEAD_DIM)
    for h in range(n_heads):
        q_t = qt_ref[h]

        def body(kt, carry, h=h, q_t=q_t):
            m, l, acc = carry
            k0 = pl.multiple_of(kt * tq, tq)
            k_t = k_ref[pl.ds(k0, tq), h * HEAD_DIM:(h + 1) * HEAD_DIM]
            s = jnp.dot(k_t, q_t, preferred_element_type=F32) * scale
            s = s + bias_ref[pl.ds(k0, tq), :].astype(F32)
            m_new = jnp.maximum(m, s.max(axis=0, keepdims=True))
            alpha = jnp.exp(m - m_new)
            p = jnp.exp(s - m_new)
            l = alpha * l + p.sum(axis=0, keepdims=True)
            acc = acc * alpha + jnp.dot(vt_ref[h, kt], p.astype(BF16), preferred_element_type=F32)
            return m_new, l, acc

        init = (jnp.full((1, tq), -jnp.inf, F32), jnp.zeros((1, tq), F32),
                jnp.zeros((HEAD_DIM, tq), F32))
        _, l, acc = lax.fori_loop(0, qb + 1, body, init)
        ot_ref[h] = (acc / l).astype(ot_ref.dtype)


def _sparse_attn(qat, h3, vat, bias, *, tq, n_heads, k_col_block):
    bsz, s_len, _ = h3.shape
    d_a = n_heads * HEAD_DIM
    n_kt = s_len // tq
    return pl.pallas_call(
        functools.partial(_sparse_attn_kernel, tq=tq, n_heads=n_heads),
        grid=(bsz, s_len // tq),
        in_specs=[pl.BlockSpec((None, n_heads, HEAD_DIM, tq), lambda b, q: (b, 0, 0, q)),
                  pl.BlockSpec((None, s_len, d_a), lambda b, q: (b, 0, k_col_block)),
                  pl.BlockSpec((None, n_heads, n_kt, HEAD_DIM, tq), lambda b, q: (b, 0, 0, 0, 0)),
                  pl.BlockSpec((None, s_len, tq), lambda b, q: (b, 0, q))],
        out_specs=pl.BlockSpec((None, n_heads, HEAD_DIM, tq), lambda b, q: (b, 0, 0, q)),
        out_shape=jax.ShapeDtypeStruct((bsz, n_heads, HEAD_DIM, s_len), BF16),
        compiler_params=_params(("parallel", "arbitrary")),
        name="sparse_attn",
    )(qat, h3, vat, bias)


def _stick_kernel(q_ref, k_ref, v_ref, tri_ref, o_ref, *, tq, n_heads):
    qb = pl.program_id(1)
    scale = 1.0 / math.sqrt(HEAD_DIM)
    q_pos = qb * tq + lax.broadcasted_iota(I32, (tq, tq), 0)
    k_local = lax.broadcasted_iota(I32, (tq, tq), 1)
    tri = tri_ref[...]
    for h in range(n_heads):
        q = q_ref[:, h * HEAD_DIM:(h + 1) * HEAD_DIM]

        def body(i, carry, h=h, q=q):
            run, acc = carry
            kt = qb - i
            k0 = pl.multiple_of(kt * tq, tq)
            k_t = k_ref[pl.ds(k0, tq), h * HEAD_DIM:(h + 1) * HEAD_DIM]
            v_t = v_ref[pl.ds(k0, tq), h * HEAD_DIM:(h + 1) * HEAD_DIM]
            z = lax.dot_general(q, k_t, (((1,), (1,)), ((), ())),
                                preferred_element_type=F32) * scale
            strict = (k0 + k_local) < q_pos
            softplus = jnp.maximum(z, 0.0) + jnp.log(1.0 + jnp.exp(-jnp.abs(z)))
            log_keep = jnp.where(strict, -softplus, 0.0)
            hi = log_keep.astype(BF16)
            lo = (log_keep - hi.astype(F32)).astype(BF16)
            suffix = (jnp.dot(hi, tri, preferred_element_type=F32)
                      + jnp.dot(lo, tri, preferred_element_type=F32))
            a = jnp.where(strict, jnp.exp(z + suffix + run), 0.0)
            acc = acc + jnp.dot(a.astype(BF16), v_t, preferred_element_type=F32)
            return run + suffix[:, 0:1], acc

        init = (jnp.zeros((tq, 1), F32), jnp.zeros((tq, HEAD_DIM), F32))
        _, acc = lax.fori_loop(0, qb + 1, body, init)
        o_ref[:, h * HEAD_DIM:(h + 1) * HEAD_DIM] = acc.astype(o_ref.dtype)


def _stick_attn(h3, tri, *, tq, n_heads, q_col_block):
    bsz, s_len, _ = h3.shape
    d_b = n_heads * HEAD_DIM
    return pl.pallas_call(
        functools.partial(_stick_kernel, tq=tq, n_heads=n_heads),
        grid=(bsz, s_len // tq),
        in_specs=[pl.BlockSpec((None, tq, d_b), lambda b, q: (b, q, q_col_block)),
                  pl.BlockSpec((None, s_len, d_b), lambda b, q: (b, 0, q_col_block + 1)),
                  pl.BlockSpec((None, s_len, d_b), lambda b, q: (b, 0, q_col_block + 2)),
                  pl.BlockSpec((tq, tq), lambda b, q: (0, 0))],
        out_specs=pl.BlockSpec((None, tq, d_b), lambda b, q: (b, q, 0)),
        out_shape=jax.ShapeDtypeStruct((bsz, s_len, d_b), BF16),
        compiler_params=_params(("parallel", "arbitrary")),
        name="stick_attn",
    )(h3, h3, h3, tri)


def _gated_proj_kernel(oa_ref, ob_ref, pa_ref, pb_ref, ga_ref, gb_ref, o_ref):
    ya = jnp.dot(oa_ref[...], pa_ref[...], preferred_element_type=F32)
    yb = jnp.dot(ob_ref[...], pb_ref[...], preferred_element_type=F32)
    sa = 1.0 / (1.0 + jnp.exp(-ga_ref[...].astype(F32)))
    sb = 1.0 / (1.0 + jnp.exp(-gb_ref[...].astype(F32)))
    o_ref[...] = (sa * ya + sb * yb).astype(o_ref.dtype)


def _gated_proj(oa, ob, pa, pb, hmain, *, ga_col):
    n, d_a = oa.shape
    d = pa.shape[1]
    bm = _pick(n, 1024)
    bn = _pick(d, 1024)
    ga_blk = ga_col // bn
    gb_blk = (ga_col + d) // bn
    return pl.pallas_call(
        _gated_proj_kernel,
        grid=(n // bm, d // bn),
        in_specs=[pl.BlockSpec((bm, d_a), lambda i, j: (i, 0)),
                  pl.BlockSpec((bm, d_a), lambda i, j: (i, 0)),
                  pl.BlockSpec((d_a, bn), lambda i, j: (0, j)),
                  pl.BlockSpec((d_a, bn), lambda i, j: (0, j)),
                  pl.BlockSpec((bm, bn), lambda i, j: (i, ga_blk + j)),
                  pl.BlockSpec((bm, bn), lambda i, j: (i, gb_blk + j))],
        out_specs=pl.BlockSpec((bm, bn), lambda i, j: (i, j)),
        out_shape=jax.ShapeDtypeStruct((n, d), BF16),
        compiler_params=_params(("parallel", "arbitrary")),
        name="gated_proj",
    )(oa, ob, pa, pb, hmain, hmain)


def _mm_res_ln_kernel(a_ref, w_ref, x_ref, g_ref, b_ref, of_ref, ob_ref, acc_ref, *, alpha):
    k = pl.program_id(1)
    part = jnp.dot(a_ref[...], w_ref[...], preferred_element_type=F32)

    @pl.when(k == 0)
    def _():
        acc_ref[...] = part

    @pl.when(k > 0)
    def _():
        acc_ref[...] += part

    @pl.when(k == pl.num_programs(1) - 1)
    def _():
        y = _layer_norm(alpha * x_ref[...] + acc_ref[...], g_ref[...], b_ref[...])
        of_ref[...] = y
        ob_ref[...] = y.astype(BF16)


def _mm_res_ln(a, w3, x, g, b, *, alpha, bk):
    n, k_total = a.shape
    n_e, kk, d = w3.shape
    assert k_total == n_e * kk and kk % bk == 0
    per_e = kk // bk
    bm = _pick(n, 512)
    return pl.pallas_call(
        functools.partial(_mm_res_ln_kernel, alpha=alpha),
        grid=(n // bm, k_total // bk),
        in_specs=[pl.BlockSpec((bm, bk), lambda i, k: (i, k)),
                  pl.BlockSpec((None, bk, d), lambda i, k: (k // per_e, k % per_e, 0)),
                  pl.BlockSpec((bm, d), lambda i, k: (i, 0)),
                  pl.BlockSpec((1, d), lambda i, k: (0, 0)),
                  pl.BlockSpec((1, d), lambda i, k: (0, 0))],
        out_specs=[pl.BlockSpec((bm, d), lambda i, k: (i, 0)),
                   pl.BlockSpec((bm, d), lambda i, k: (i, 0))],
        out_shape=[jax.ShapeDtypeStruct((n, d), F32), jax.ShapeDtypeStruct((n, d), BF16)],
        scratch_shapes=[pltpu.VMEM((bm, d), F32)],
        compiler_params=_params(("parallel", "arbitrary")),
        name="mm_res_ln",
    )(a, w3, x, g.reshape(1, d), b.reshape(1, d))


def _swiglu_kernel(x_ref, wg_ref, wu_ref, comb_ref, o_ref, *, use_comb):
    x = x_ref[...]
    g = jnp.dot(x, wg_ref[...], preferred_element_type=F32)
    u = jnp.dot(x, wu_ref[...], preferred_element_type=F32)
    hid = g / (1.0 + jnp.exp(-g)) * u
    if use_comb:
        e = pl.program_id(1)
        comb = comb_ref[...]
        lane = lax.broadcasted_iota(I32, comb.shape, 1)
        hid = hid * jnp.sum(jnp.where(lane == e, comb, 0.0), axis=1, keepdims=True)
    o_ref[...] = hid.astype(o_ref.dtype)


def _swiglu_hidden(xb, wg3, wu3, comb):
    n, d = xb.shape
    n_e, _, f = wg3.shape
    bm = _pick(n, 1024)
    bn = _pick(f, 512)
    per_e = f // bn
    use_comb = comb is not None
    if comb is None:
        comb = jnp.zeros((n, LANES), F32)
    return pl.pallas_call(
        functools.partial(_swiglu_kernel, use_comb=use_comb),
        grid=(n // bm, n_e, per_e),
        in_specs=[pl.BlockSpec((bm, d), lambda i, e, j: (i, 0)),
                  pl.BlockSpec((None, d, bn), lambda i, e, j: (e, 0, j)),
                  pl.BlockSpec((None, d, bn), lambda i, e, j: (e, 0, j)),
                  pl.BlockSpec((bm, LANES), lambda i, e, j: (i, 0))],
        out_specs=pl.BlockSpec((bm, bn), lambda i, e, j: (i, e * per_e + j)),
        out_shape=jax.ShapeDtypeStruct((n, n_e * f), BF16),
        compiler_params=_params(("parallel", "arbitrary", "arbitrary")),
        name="swiglu_hidden",
    )(xb, wg3, wu3, comb)


def _router_kernel(x_ref, w_ref, comb_ref, *, n_experts):
    logits = jnp.dot(x_ref[...], w_ref[...], preferred_element_type=F32)
    lane = lax.broadcasted_iota(I32, logits.shape, 1).astype(F32)
    logits = jnp.where(lane < n_experts, logits, -jnp.inf)
    v1 = jnp.max(logits, axis=1, keepdims=True)
    i1 = jnp.min(jnp.where(logits == v1, lane, float(LANES)), axis=1, keepdims=True)
    rest = jnp.where(lane == i1, -jnp.inf, logits)
    v2 = jnp.max(rest, axis=1, keepdims=True)
    i2 = jnp.min(jnp.where(rest == v2, lane, float(LANES)), axis=1, keepdims=True)
    e2 = jnp.exp(v2 - v1)
    g1 = 1.0 / (1.0 + e2)
    g2 = e2 / (1.0 + e2)
    comb_ref[...] = jnp.where(lane == i1, g1, 0.0) + jnp.where(lane == i2, g2, 0.0)


def _router(xb, rw, *, n_experts):
    n, d = xb.shape
    bm = _pick(n, 1024)
    return pl.pallas_call(
        functools.partial(_router_kernel, n_experts=n_experts),
        grid=(n // bm,),
        in_specs=[pl.BlockSpec((bm, d), lambda i: (i, 0)),
                  pl.BlockSpec((d, LANES), lambda i: (0, 0))],
        out_specs=pl.BlockSpec((bm, LANES), lambda i: (i, 0)),
        out_shape=jax.ShapeDtypeStruct((n, LANES), F32),
        compiler_params=_params(("parallel",)),
        name="router",
    )(xb, rw)


def _rope_tables(positions):
    pos = positions.astype(F32)[..., None]
    inv_h = ROPE_THETA ** (-jnp.arange(0, HEAD_DIM, 2, dtype=F32) / HEAD_DIM)
    inv_i = ROPE_THETA ** (-jnp.arange(0, IDX_DIM, 2, dtype=F32) / IDX_DIM)
    ang_h = pos * inv_h
    ang_i = pos * inv_i
    ch, sh = jnp.cos(ang_h), jnp.sin(ang_h)
    ci, si = jnp.cos(ang_i), jnp.sin(ang_i)
    z32 = jnp.zeros_like(si)
    one64 = jnp.ones(ci.shape[:-1] + (IDX_DIM,), F32)
    z64 = jnp.zeros_like(one64)
    cat = lambda *a: jnp.concatenate(a, axis=-1).reshape(-1, LANES)
    main = (cat(ch, ch), cat(-sh, sh), cat(ci, ci, ci, ci), cat(-si, z32, -si, z32),
            cat(z32, si, z32, si))
    small = (main[0], main[1], cat(ci, ci, one64), cat(-si, z32, z64), cat(z32, si, z64))
    return main, small


def kernel(x, positions, ln_in_g, ln_in_b, w_in, w_proj_a, w_proj_b, w_out, ln_mix_g, ln_mix_b,
           ffn_w_gate, ffn_w_up, ffn_w_down, router_w, moe_w_gate, moe_w_up, moe_w_down,
           ln_ffn_g, ln_ffn_b):
    bsz, s_len, d = x.shape
    depth = w_in.shape[0]
    n = bsz * s_len
    n_heads = d // (2 * HEAD_DIM)
    d_a = n_heads * HEAD_DIM
    d_qi = IDX_HEADS * IDX_DIM
    n_experts = router_w.shape[-1]
    alpha = (2.0 * depth) ** 0.25
    n_sel = min(TOPK_MAX, s_len // 4)
    tq = 256
    assert s_len % tq == 0 and tq % CHUNK == 0 and n_sel <= tq and d_qi % d_a == 0

    off_ki = 3 * d_a + d_qi
    off_qb = off_ki + IDX_DIM + IDX_HEADS
    qb_col_block = (3 * d_a + d_qi) // d_a
    ga_col = 6 * d_a + d_qi

    tabs_main, tabs_small = _rope_tables(positions)
    tri = (lax.broadcasted_iota(I32, (tq, tq), 0) >= lax.broadcasted_iota(I32, (tq, tq), 1)
           ).astype(BF16)

    xf, xb = _input_ln(x.reshape(n, d), ln_in_g, ln_in_b)
    for layer in range(depth):
        w_l = w_in[layer]
        w_main = jnp.concatenate([w_l[:, :off_ki], w_l[:, off_qb:]], axis=1).astype(BF16)
        w_small = jnp.pad(w_l[:, off_ki:off_qb],
                          ((0, 0), (0, LANES - IDX_DIM - IDX_HEADS))).astype(BF16)
        hmain = _inproj(xb, w_main, tabs_main, bn=d_a, rope128_tiles=2,
                        rope64_tiles=(3, 3 + d_qi // d_a), out_dtype=BF16)
        hsmall = _inproj(xb, w_small, tabs_small, bn=LANES, rope128_tiles=0,
                         rope64_tiles=(0, 1), out_dtype=F32)
        h3 = hmain.reshape(bsz, s_len, -1)
        hs3 = hsmall.reshape(bsz, s_len, LANES)

        qat = h3[..., :d_a].reshape(bsz, s_len, n_heads, HEAD_DIM).transpose(0, 2, 3, 1)
        vat = (h3[..., 2 * d_a:3 * d_a].reshape(bsz, s_len // tq, tq, n_heads, HEAD_DIM)
               .transpose(0, 3, 1, 4, 2))
        qit = (h3[..., 3 * d_a:3 * d_a + d_qi].reshape(bsz, s_len, IDX_HEADS, IDX_DIM)
               .transpose(0, 2, 3, 1))
        ki = hs3[..., :IDX_DIM].astype(BF16)
        wit = hs3[..., IDX_DIM:IDX_DIM + IDX_HEADS].transpose(0, 2, 1)

        bias = _indexer(qit, ki, wit, tq=tq, n_sel=n_sel)
        oat = _sparse_attn(qat, h3, vat, bias, tq=tq, n_heads=n_heads, k_col_block=1)
        o_a = oat.transpose(0, 3, 1, 2).reshape(n, d_a)
        o_b = _stick_attn(h3, tri, tq=tq, n_heads=n_heads, q_col_block=qb_col_block
                          ).reshape(n, d_a)

        merged = _gated_proj(o_a, o_b, w_proj_a[layer].astype(BF16), w_proj_b[layer].astype(BF16),
                             hmain, ga_col=ga_col)
        xf, xb = _mm_res_ln(merged, w_out[layer].astype(BF16)[None], xf, ln_mix_g[layer],
                            ln_mix_b[layer], alpha=alpha, bk=_pick(d, 1024))

        i = layer // 2
        if layer % 2 == 0:
            hid = _swiglu_hidden(xb, ffn_w_gate[i].astype(BF16)[None],
                                 ffn_w_up[i].astype(BF16)[None], None)
            w_down = ffn_w_down[i].astype(BF16)[None]
        else:
            rw = jnp.pad(router_w[i], ((0, 0), (0, LANES - n_experts))).astype(BF16)
            comb = _router(xb, rw, n_experts=n_experts)
            hid = _swiglu_hidden(xb, moe_w_gate[i].astype(BF16), moe_w_up[i].astype(BF16), comb)
            w_down = moe_w_down[i].astype(BF16)
        f_dim = w_down.shape[1]
        bk = max(b for b in (1408, 1024, 512, 256, 128) if f_dim % b == 0)
        xf, xb = _mm_res_ln(hid, w_down, xf, ln_ffn_g[layer], ln_ffn_b[layer], alpha=alpha, bk=bk)
    return xf.reshape(bsz, s_len, d)
```

```python
import functools
import math

import jax
import jax.numpy as jnp
from jax import lax
from jax.experimental import pallas as pl
from jax.experimental.pallas import tpu as pltpu

F32 = jnp.float32
BF16 = jnp.bfloat16
I32 = jnp.int32

LN_EPS = 1e-5
HEAD_DIM = 128
IDX_HEADS = 16
IDX_DIM = 64
CHUNK = 64
TOPK_MAX = 256
ROPE_THETA = 10000.0
TOP_K_EXPERTS = 2
LANES = 128
VMEM_LIMIT = 56 * 1024 * 1024
MASK_NEG = -1e30
KEY_NEG_INF = -2139095041
STICK_SKIP_LOG = -100.0
INT_MIN = -2147483648


def _params(sem):
    return pltpu.CompilerParams(dimension_semantics=sem, vmem_limit_bytes=VMEM_LIMIT)


def _pick(n, pref):
    b = min(n, pref)
    while n % b:
        b //= 2
    return b


def _layer_norm(v, g, b):
    mu = jnp.mean(v, axis=-1, keepdims=True)
    c = v - mu
    var = jnp.mean(c * c, axis=-1, keepdims=True)
    return c * lax.rsqrt(var + LN_EPS) * g + b


def _ln_kernel(x_ref, g_ref, b_ref, of_ref, ob_ref):
    y = _layer_norm(x_ref[...], g_ref[...], b_ref[...])
    of_ref[...] = y
    ob_ref[...] = y.astype(BF16)


def _input_ln(x2d, g, b):
    n, d = x2d.shape
    bm = _pick(n, 512)
    return pl.pallas_call(
        _ln_kernel,
        grid=(n // bm,),
        in_specs=[pl.BlockSpec((bm, d), lambda i: (i, 0)),
                  pl.BlockSpec((1, d), lambda i: (0, 0)),
                  pl.BlockSpec((1, d), lambda i: (0, 0))],
        out_specs=[pl.BlockSpec((bm, d), lambda i: (i, 0)),
                   pl.BlockSpec((bm, d), lambda i: (i, 0))],
        out_shape=[jax.ShapeDtypeStruct((n, d), F32), jax.ShapeDtypeStruct((n, d), BF16)],
        compiler_params=_params(("parallel",)),
        name="input_ln",
    )(x2d, g.reshape(1, d), b.reshape(1, d))


def _inproj_kernel(x_ref, w_ref, c128_ref, s128_ref, c64_ref, sa64_ref, sb64_ref, o_ref, *,
                   rope128_tiles, rope64_tiles):
    j = pl.program_id(1)
    y = jnp.dot(x_ref[...], w_ref[...], preferred_element_type=F32)
    n_heads = y.shape[1] // LANES
    lo64, hi64 = rope64_tiles
    is128 = j < rope128_tiles
    is64 = jnp.logical_and(j >= lo64, j < hi64)

    @pl.when(is128)
    def _():
        c = c128_ref[...]
        s = s128_ref[...]
        for h in range(n_heads):
            yh = y[:, h * LANES:(h + 1) * LANES]
            o_ref[:, h * LANES:(h + 1) * LANES] = (
                yh * c + pltpu.roll(yh, HEAD_DIM // 2, 1) * s).astype(o_ref.dtype)

    @pl.when(is64)
    def _():
        c = c64_ref[...]
        sa = sa64_ref[...]
        sb = sb64_ref[...]
        for h in range(n_heads):
            yh = y[:, h * LANES:(h + 1) * LANES]
            o_ref[:, h * LANES:(h + 1) * LANES] = (
                yh * c + pltpu.roll(yh, LANES - IDX_DIM // 2, 1) * sa
                + pltpu.roll(yh, IDX_DIM // 2, 1) * sb).astype(o_ref.dtype)

    @pl.when(jnp.logical_not(jnp.logical_or(is128, is64)))
    def _():
        o_ref[...] = y.astype(o_ref.dtype)


def _inproj(xb, w, tabs, *, bn, rope128_tiles, rope64_tiles, out_dtype):
    n, d = xb.shape
    cols = w.shape[1]
    bm = _pick(n, 1024)
    tab_spec = pl.BlockSpec((bm, LANES), lambda i, j: (i, 0))
    return pl.pallas_call(
        functools.partial(_inproj_kernel, rope128_tiles=rope128_tiles, rope64_tiles=rope64_tiles),
        grid=(n // bm, cols // bn),
        in_specs=[pl.BlockSpec((bm, d), lambda i, j: (i, 0)),
                  pl.BlockSpec((d, bn), lambda i, j: (0, j))] + [tab_spec] * 5,
        out_specs=pl.BlockSpec((bm, bn), lambda i, j: (i, j)),
        out_shape=jax.ShapeDtypeStruct((n, cols), out_dtype),
        compiler_params=_params(("parallel", "arbitrary")),
        name="inproj",
    )(xb, w, *tabs)


def _indexer_kernel(qit_ref, ki_ref, wit_ref, bias_ref, skey_ref, *, tq, n_sel):
    qb = pl.program_id(1)
    n_kt = bias_ref.shape[0] // tq
    q_chunk = (qb * tq + lax.broadcasted_iota(I32, (tq, tq), 1)) // CHUNK
    k_local = lax.broadcasted_iota(I32, (tq, tq), 0)

    def score_tile(kt, carry):
        k0 = pl.multiple_of(kt * tq, tq)
        k_t = ki_ref[pl.ds(k0, tq), :]
        acc = jnp.zeros((tq, tq), F32)
        for h in range(IDX_HEADS):
            lg = jnp.dot(k_t, qit_ref[h], preferred_element_type=F32)
            acc = acc + jnp.maximum(lg, 0.0) * wit_ref[h:h + 1, :]
        acc = jnp.where((k0 + k_local) // CHUNK <= q_chunk, acc, -jnp.inf)
        bits = pltpu.bitcast(acc, I32)
        skey_ref[pl.ds(k0, tq), :] = bits ^ ((bits >> 31) & 0x7FFFFFFF)
        return carry

    lax.fori_loop(0, qb + 1, score_tile, 0)

    def count_ge(cand):
        def body(kt, acc8):
            k0 = pl.multiple_of(kt * tq, tq)
            m = jnp.where(skey_ref[pl.ds(k0, tq), :] >= cand, 1, 0).astype(I32)
            return acc8 + m.reshape(tq // 8, 8, tq).sum(axis=0)
        acc8 = lax.fori_loop(0, qb + 1, body, jnp.zeros((8, tq), I32))
        return acc8.sum(axis=0, keepdims=True)

    def bit_step(i, ans):
        cand = ans + lax.shift_left(jnp.int32(1), 31 - i)
        return jnp.where(count_ge(cand) >= n_sel, cand, ans)

    n_bits = jnp.where((qb + 1) * tq > n_sel, 32, 0)
    thr = lax.fori_loop(0, n_bits, bit_step, jnp.full((1, tq), INT_MIN, I32))
    thr = jnp.maximum(thr, KEY_NEG_INF + 1)

    def bias_tile(kt, carry):
        k0 = pl.multiple_of(kt * tq, tq)
        sel = skey_ref[pl.ds(k0, tq), :] >= thr
        bias_ref[pl.ds(k0, tq), :] = jnp.where(sel, 0.0, MASK_NEG).astype(BF16)
        return carry

    lax.fori_loop(0, qb + 1, bias_tile, 0)

    def fill_tile(kt, carry):
        k0 = pl.multiple_of(kt * tq, tq)
        bias_ref[pl.ds(k0, tq), :] = jnp.full((tq, tq), MASK_NEG, BF16)
        return carry

    lax.fori_loop(qb + 1, n_kt, fill_tile, 0)


def _indexer(qit, ki, wit, *, tq, n_sel):
    bsz, _, _, s_len = qit.shape
    return pl.pallas_call(
        functools.partial(_indexer_kernel, tq=tq, n_sel=n_sel),
        grid=(bsz, s_len // tq),
        in_specs=[pl.BlockSpec((None, IDX_HEADS, IDX_DIM, tq), lambda b, q: (b, 0, 0, q)),
                  pl.BlockSpec((None, s_len, IDX_DIM), lambda b, q: (b, 0, 0)),
                  pl.BlockSpec((None, IDX_HEADS, tq), lambda b, q: (b, 0, q))],
        out_specs=pl.BlockSpec((None, s_len, tq), lambda b, q: (b, 0, q)),
        out_shape=jax.ShapeDtypeStruct((bsz, s_len, s_len), BF16),
        scratch_shapes=[pltpu.VMEM((s_len, tq), I32)],
        compiler_params=_params(("parallel", "arbitrary")),
        name="indexer_topk",
    )(qit, ki, wit)


def _sparse_attn_kernel(qt_ref, k_ref, vt_ref, bias_ref, ot_ref, acc_ref, s_ref, p_ref, *, tq,
                        n_heads):
    qb = pl.program_id(1)
    c = math.log2(math.e) / math.sqrt(HEAD_DIM)
    acc_ref[...] = jnp.zeros(acc_ref.shape, F32)

    def body(kt, carry):
        ms, ls = carry
        k0 = pl.multiple_of(kt * tq, tq)
        for h in range(n_heads):
            k_t = k_ref[pl.ds(k0, tq), h * HEAD_DIM:(h + 1) * HEAD_DIM]
            s_ref[h] = jnp.dot(k_t, qt_ref[h], preferred_element_type=F32)
        bias = bias_ref[pl.ds(k0, tq), :].astype(F32)
        new_ms, new_ls, alphas = [], [], []
        for h in range(n_heads):
            s = s_ref[h] * c + bias
            m_new = jnp.maximum(ms[h], s.max(axis=0, keepdims=True))
            alpha = jnp.exp2(ms[h] - m_new)
            p = jnp.exp2(s - m_new)
            p_ref[h] = p.astype(BF16)
            new_ms.append(m_new)
            new_ls.append(alpha * ls[h] + p.sum(axis=0, keepdims=True))
            alphas.append(alpha)
        for h in range(n_heads):
            acc_ref[h] = acc_ref[h] * alphas[h] + jnp.dot(vt_ref[h, kt], p_ref[h],
                                                          preferred_element_type=F32)
        return tuple(new_ms), tuple(new_ls)

    init = (tuple(jnp.full((1, tq), -jnp.inf, F32) for _ in range(n_heads)),
            tuple(jnp.zeros((1, tq), F32) for _ in range(n_heads)))
    _, ls = lax.fori_loop(0, qb + 1, body, init)
    for h in range(n_heads):
        ot_ref[h] = (acc_ref[h] / ls[h]).astype(ot_ref.dtype)


def _sparse_attn(qat, h3, vat, bias, *, tq, n_heads, k_col_block):
    bsz, s_len, _ = h3.shape
    d_a = n_heads * HEAD_DIM
    n_kt = s_len // tq
    return pl.pallas_call(
        functools.partial(_sparse_attn_kernel, tq=tq, n_heads=n_heads),
        grid=(bsz, s_len // tq),
        in_specs=[pl.BlockSpec((None, n_heads, HEAD_DIM, tq), lambda b, q: (b, 0, 0, q)),
                  pl.BlockSpec((None, s_len, d_a), lambda b, q: (b, 0, k_col_block)),
                  pl.BlockSpec((None, n_heads, n_kt, HEAD_DIM, tq), lambda b, q: (b, 0, 0, 0, 0)),
                  pl.BlockSpec((None, s_len, tq), lambda b, q: (b, 0, q))],
        out_specs=pl.BlockSpec((None, n_heads, HEAD_DIM, tq), lambda b, q: (b, 0, 0, q)),
        out_shape=jax.ShapeDtypeStruct((bsz, n_heads, HEAD_DIM, s_len), BF16),
        scratch_shapes=[pltpu.VMEM((n_heads, HEAD_DIM, tq), F32),
                        pltpu.VMEM((n_heads, tq, tq), F32),
                        pltpu.VMEM((n_heads, tq, tq), BF16)],
        compiler_params=_params(("parallel", "arbitrary")),
        name="sparse_attn",
    )(qat, h3, vat, bias)


def _stick_kernel(q_ref, k_ref, v_ref, tri_ref, o_ref, kmax_ref, acc_ref, run_ref, z_ref, suf_ref,
                  hi_ref, lo_ref, a_ref, *, tq, n_heads):
    qb = pl.program_id(1)
    s_len, d_b = k_ref.shape
    scale = 1.0 / math.sqrt(HEAD_DIM)
    heads = [slice(h * HEAD_DIM, (h + 1) * HEAD_DIM) for h in range(n_heads)]

    @pl.when(qb == 0)
    def _():
        def body(kt, m8):
            k_t = jnp.abs(k_ref[pl.ds(pl.multiple_of(kt * tq, tq), tq), :].astype(F32))
            return jnp.maximum(m8, k_t.reshape(tq // 8, 8, d_b).max(axis=0))
        m8 = lax.fori_loop(0, s_len // tq, body, jnp.zeros((8, d_b), F32))
        kmax_ref[...] = m8.max(axis=0, keepdims=True)

    z_bounds = [jnp.sum(jnp.abs(q_ref[:, c].astype(F32)) * kmax_ref[:, c], axis=1, keepdims=True)
                * scale for c in heads]
    tri = tri_ref[...]

    def tile_step(k0, strict):
        for h, c in enumerate(heads):
            z_ref[h] = lax.dot_general(q_ref[:, c], k_ref[pl.ds(k0, tq), c],
                                       (((1,), (1,)), ((), ())), preferred_element_type=F32)
        for h in range(n_heads):
            z = z_ref[h] * scale
            log_keep = -(jnp.maximum(z, 0.0) + jnp.log(1.0 + jnp.exp(-jnp.abs(z))))
            if strict is not None:
                log_keep = jnp.where(strict, log_keep, 0.0)
            hi = log_keep.astype(BF16)
            hi_ref[h] = hi
            lo_ref[h] = (log_keep - hi.astype(F32)).astype(BF16)
        for h in range(n_heads):
            suf_ref[h] = (jnp.dot(hi_ref[h], tri, preferred_element_type=F32)
                          + jnp.dot(lo_ref[h], tri, preferred_element_type=F32))
        bound = None
        for h in range(n_heads):
            suffix = suf_ref[h]
            a = jnp.exp(z_ref[h] * scale + suffix + run_ref[h])
            if strict is not None:
                a = jnp.where(strict, a, 0.0)
            a_ref[h] = a.astype(BF16)
            run = run_ref[h] + suffix[:, 0:1]
            run_ref[h] = run
            b_h = jnp.max(run + z_bounds[h])
            bound = b_h if bound is None else jnp.maximum(bound, b_h)
        for h, c in enumerate(heads):
            acc_ref[:, c] += jnp.dot(a_ref[h], v_ref[pl.ds(k0, tq), c],
                                     preferred_element_type=F32)
        return bound

    acc_ref[...] = jnp.zeros(acc_ref.shape, F32)
    run_ref[...] = jnp.zeros(run_ref.shape, F32)
    strict = lax.broadcasted_iota(I32, (tq, tq), 1) < lax.broadcasted_iota(I32, (tq, tq), 0)
    bound0 = tile_step(pl.multiple_of(qb * tq, tq), strict)

    def cond(carry):
        i, bound = carry
        return jnp.logical_and(i <= qb, bound > STICK_SKIP_LOG)

    def body(carry):
        i, _ = carry
        return i + 1, tile_step(pl.multiple_of((qb - i) * tq, tq), None)

    lax.while_loop(cond, body, (jnp.int32(1), bound0))
    o_ref[...] = acc_ref[...].astype(o_ref.dtype)


def _stick_attn(h3, tri, *, tq, n_heads, q_col_block):
    bsz, s_len, _ = h3.shape
    d_b = n_heads * HEAD_DIM
    return pl.pallas_call(
        functools.partial(_stick_kernel, tq=tq, n_heads=n_heads),
        grid=(bsz, s_len // tq),
        in_specs=[pl.BlockSpec((None, tq, d_b), lambda b, q: (b, q, q_col_block)),
                  pl.BlockSpec((None, s_len, d_b), lambda b, q: (b, 0, q_col_block + 1)),
                  pl.BlockSpec((None, s_len, d_b), lambda b, q: (b, 0, q_col_block + 2)),
                  pl.BlockSpec((tq, tq), lambda b, q: (0, 0))],
        out_specs=pl.BlockSpec((None, tq, d_b), lambda b, q: (b, q, 0)),
        out_shape=jax.ShapeDtypeStruct((bsz, s_len, d_b), BF16),
        scratch_shapes=[pltpu.VMEM((1, d_b), F32),
                        pltpu.VMEM((tq, d_b), F32),
                        pltpu.VMEM((n_heads, tq, 1), F32),
                        pltpu.VMEM((n_heads, tq, tq), F32),
                        pltpu.VMEM((n_heads, tq, tq), F32),
                        pltpu.VMEM((n_heads, tq, tq), BF16),
                        pltpu.VMEM((n_heads, tq, tq), BF16),
                        pltpu.VMEM((n_heads, tq, tq), BF16)],
        compiler_params=_params(("arbitrary", "arbitrary")),
        name="stick_attn",
    )(h3, h3, h3, tri)


def _gated_proj_kernel(oa_ref, ob_ref, pa_ref, pb_ref, ga_ref, gb_ref, o_ref):
    ya = jnp.dot(oa_ref[...], pa_ref[...], preferred_element_type=F32)
    yb = jnp.dot(ob_ref[...], pb_ref[...], preferred_element_type=F32)
    sa = 1.0 / (1.0 + jnp.exp(-ga_ref[...].astype(F32)))
    sb = 1.0 / (1.0 + jnp.exp(-gb_ref[...].astype(F32)))
    o_ref[...] = (sa * ya + sb * yb).astype(o_ref.dtype)


def _gated_proj(oa, ob, pa, pb, hmain, *, ga_col):
    n, d_a = oa.shape
    d = pa.shape[1]
    bm = _pick(n, 1024)
    bn = _pick(d, 1024)
    ga_blk = ga_col // bn
    gb_blk = (ga_col + d) // bn
    return pl.pallas_call(
        _gated_proj_kernel,
        grid=(n // bm, d // bn),
        in_specs=[pl.BlockSpec((bm, d_a), lambda i, j: (i, 0)),
                  pl.BlockSpec((bm, d_a), lambda i, j: (i, 0)),
                  pl.BlockSpec((d_a, bn), lambda i, j: (0, j)),
                  pl.BlockSpec((d_a, bn), lambda i, j: (0, j)),
                  pl.BlockSpec((bm, bn), lambda i, j: (i, ga_blk + j)),
                  pl.BlockSpec((bm, bn), lambda i, j: (i, gb_blk + j))],
        out_specs=pl.BlockSpec((bm, bn), lambda i, j: (i, j)),
        out_shape=jax.ShapeDtypeStruct((n, d), BF16),
        compiler_params=_params(("parallel", "arbitrary")),
        name="gated_proj",
    )(oa, ob, pa, pb, hmain, hmain)


def _mm_res_ln_kernel(a_ref, w_ref, x_ref, g_ref, b_ref, of_ref, ob_ref, acc_ref, *, alpha):
    k = pl.program_id(1)
    part = jnp.dot(a_ref[...], w_ref[...], preferred_element_type=F32)

    @pl.when(k == 0)
    def _():
        acc_ref[...] = part

    @pl.when(k > 0)
    def _():
        acc_ref[...] += part

    @pl.when(k == pl.num_programs(1) - 1)
    def _():
        y = _layer_norm(alpha * x_ref[...] + acc_ref[...], g_ref[...], b_ref[...])
        of_ref[...] = y
        ob_ref[...] = y.astype(BF16)


def _mm_res_ln(a, w3, x, g, b, *, alpha, bk):
    n, k_total = a.shape
    n_e, kk, d = w3.shape
    assert k_total == n_e * kk and kk % bk == 0
    per_e = kk // bk
    bm = _pick(n, 512)
    return pl.pallas_call(
        functools.partial(_mm_res_ln_kernel, alpha=alpha),
        grid=(n // bm, k_total // bk),
        in_specs=[pl.BlockSpec((bm, bk), lambda i, k: (i, k)),
                  pl.BlockSpec((None, bk, d), lambda i, k: (k // per_e, k % per_e, 0)),
                  pl.BlockSpec((bm, d), lambda i, k: (i, 0)),
                  pl.BlockSpec((1, d), lambda i, k: (0, 0)),
                  pl.BlockSpec((1, d), lambda i, k: (0, 0))],
        out_specs=[pl.BlockSpec((bm, d), lambda i, k: (i, 0)),
                   pl.BlockSpec((bm, d), lambda i, k: (i, 0))],
        out_shape=[jax.ShapeDtypeStruct((n, d), F32), jax.ShapeDtypeStruct((n, d), BF16)],
        scratch_shapes=[pltpu.VMEM((bm, d), F32)],
        compiler_params=_params(("parallel", "arbitrary")),
        name="mm_res_ln",
    )(a, w3, x, g.reshape(1, d), b.reshape(1, d))


def _swiglu(x, wg, wu):
    g = jnp.dot(x, wg, preferred_element_type=F32)
    u = jnp.dot(x, wu, preferred_element_type=F32)
    return g / (1.0 + jnp.exp(-g)) * u


def _swiglu_kernel(x_ref, wg_ref, wu_ref, o_ref):
    o_ref[...] = _swiglu(x_ref[...], wg_ref[...], wu_ref[...]).astype(o_ref.dtype)


def _swiglu_hidden(xb, wg, wu):
    n, d = xb.shape
    f = wg.shape[1]
    bm = _pick(n, 1024)
    bn = _pick(f, 512)
    return pl.pallas_call(
        _swiglu_kernel,
        grid=(n // bm, f // bn),
        in_specs=[pl.BlockSpec((bm, d), lambda i, j: (i, 0)),
                  pl.BlockSpec((d, bn), lambda i, j: (0, j)),
                  pl.BlockSpec((d, bn), lambda i, j: (0, j))],
        out_specs=pl.BlockSpec((bm, bn), lambda i, j: (i, j)),
        out_shape=jax.ShapeDtypeStruct((n, f), BF16),
        compiler_params=_params(("parallel", "arbitrary")),
        name="swiglu_hidden",
    )(xb, wg, wu)


def _router_kernel(x_ref, w_ref, comb_ref, *, n_experts):
    logits = jnp.dot(x_ref[...], w_ref[...], preferred_element_type=F32)
    lane = lax.broadcasted_iota(I32, logits.shape, 1).astype(F32)
    logits = jnp.where(lane < n_experts, logits, -jnp.inf)
    v1 = jnp.max(logits, axis=1, keepdims=True)
    i1 = jnp.min(jnp.where(logits == v1, lane, float(LANES)), axis=1, keepdims=True)
    rest = jnp.where(lane == i1, -jnp.inf, logits)
    v2 = jnp.max(rest, axis=1, keepdims=True)
    i2 = jnp.min(jnp.where(rest == v2, lane, float(LANES)), axis=1, keepdims=True)
    e2 = jnp.exp(v2 - v1)
    g1 = 1.0 / (1.0 + e2)
    g2 = e2 / (1.0 + e2)
    comb_ref[...] = jnp.where(lane == 0.0, i1, jnp.where(lane == 1.0, i2, jnp.where(
        lane == 2.0, g1, jnp.where(lane == 3.0, g2, 0.0))))


def _router(xb, rw, *, n_experts):
    n, d = xb.shape
    bm = _pick(n, 1024)
    return pl.pallas_call(
        functools.partial(_router_kernel, n_experts=n_experts),
        grid=(n // bm,),
        in_specs=[pl.BlockSpec((bm, d), lambda i: (i, 0)),
                  pl.BlockSpec((d, LANES), lambda i: (0, 0))],
        out_specs=pl.BlockSpec((bm, LANES), lambda i: (i, 0)),
        out_shape=jax.ShapeDtypeStruct((n, LANES), F32),
        compiler_params=_params(("parallel",)),
        name="router",
    )(xb, rw)


def _row_gather_kernel(idx_ref, src_ref, dst_ref, sem, *, rows, window):

    def row_copy(r):
        return pltpu.make_async_copy(src_ref.at[pl.ds(idx_ref[r], 1)],
                                     dst_ref.at[pl.ds(r, 1)], sem)

    def prime(r, carry):
        row_copy(r).start()
        return carry

    def steady(r, carry):
        row_copy(r - window).wait()
        row_copy(r).start()
        return carry

    def drain(r, carry):
        row_copy(r).wait()
        return carry

    lax.fori_loop(0, window, prime, 0, unroll=8)
    lax.fori_loop(window, rows, steady, 0, unroll=8)
    lax.fori_loop(rows - window, rows, drain, 0, unroll=8)


def _row_gather(src, idx):
    n_out = idx.shape[0]
    rows = _pick(n_out, 512)
    window = 256
    assert rows >= window
    return pl.pallas_call(
        functools.partial(_row_gather_kernel, rows=rows, window=window),
        grid=(n_out // rows,),
        in_specs=[pl.BlockSpec((rows,), lambda i: (i,), memory_space=pltpu.SMEM),
                  pl.BlockSpec(memory_space=pl.ANY)],
        out_specs=pl.BlockSpec((rows,) + src.shape[1:], lambda i: (i, 0)),
        out_shape=jax.ShapeDtypeStruct((n_out,) + src.shape[1:], src.dtype),
        scratch_shapes=[pltpu.SemaphoreType.DMA(())],
        compiler_params=_params(("arbitrary",)),
        name="row_gather",
    )(idx, src)


def _moe_up_kernel(te_ref, tv_ref, x_ref, wg_ref, wu_ref, o_ref):
    t = pl.program_id(1)

    @pl.when(tv_ref[t] != 0)
    def _():
        o_ref[...] = _swiglu(x_ref[...].astype(BF16), wg_ref[...], wu_ref[...]).astype(o_ref.dtype)

    @pl.when(tv_ref[t] == 0)
    def _():
        o_ref[...] = jnp.zeros(o_ref.shape, o_ref.dtype)


def _moe_up(xs, wg, wu, tile_expert, tile_valid, *, bm):
    n_rows, d = xs.shape
    f = wg.shape[2]
    bn = _pick(f, 1024)
    grid_spec = pltpu.PrefetchScalarGridSpec(
        num_scalar_prefetch=2,
        grid=(f // bn, n_rows // bm),
        in_specs=[pl.BlockSpec((bm, d), lambda j, t, te, tv: (t, 0)),
                  pl.BlockSpec((None, d, bn), lambda j, t, te, tv: (te[t], 0, j)),
                  pl.BlockSpec((None, d, bn), lambda j, t, te, tv: (te[t], 0, j))],
        out_specs=pl.BlockSpec((bm, bn), lambda j, t, te, tv: (t, j)))
    return pl.pallas_call(
        _moe_up_kernel,
        grid_spec=grid_spec,
        out_shape=jax.ShapeDtypeStruct((n_rows, f), BF16),
        compiler_params=_params(("arbitrary", "arbitrary")),
        name="moe_up",
    )(tile_expert, tile_valid, xs, wg, wu)


def _moe_down_kernel(te_ref, tv_ref, h_ref, w_ref, o_ref):
    t = pl.program_id(0)

    @pl.when(pl.program_id(1) == 0)
    def _():
        o_ref[...] = jnp.zeros(o_ref.shape, o_ref.dtype)

    @pl.when(tv_ref[t] != 0)
    def _():
        o_ref[...] += jnp.dot(h_ref[...], w_ref[...], preferred_element_type=F32)


def _moe_down(hs, wd, tile_expert, tile_valid, *, bm):
    n_rows, f = hs.shape
    d = wd.shape[2]
    bk = _pick(f, 1024)
    grid_spec = pltpu.PrefetchScalarGridSpec(
        num_scalar_prefetch=2,
        grid=(n_rows // bm, f // bk),
        in_specs=[pl.BlockSpec((bm, bk), lambda t, k, te, tv: (t, k)),
                  pl.BlockSpec((None, bk, d), lambda t, k, te, tv: (te[t], k, 0))],
        out_specs=pl.BlockSpec((bm, d), lambda t, k, te, tv: (t, 0)))
    return pl.pallas_call(
        _moe_down_kernel,
        grid_spec=grid_spec,
        out_shape=jax.ShapeDtypeStruct((n_rows, d), F32),
        compiler_params=_params(("arbitrary", "arbitrary")),
        name="moe_down",
    )(tile_expert, tile_valid, hs, wd)


def _moe_combine_kernel(y1_ref, y2_ref, r_ref, x_ref, g_ref, b_ref, of_ref, ob_ref, *, alpha):
    route = r_ref[...]
    y = route[:, 2:3] * y1_ref[...] + route[:, 3:4] * y2_ref[...]
    out = _layer_norm(alpha * x_ref[...] + y, g_ref[...], b_ref[...])
    of_ref[...] = out
    ob_ref[...] = out.astype(BF16)


def _moe_combine(y_both, route, x, g, b, *, alpha):
    n, d = x.shape
    bm = _pick(n, 256)
    nb = n // bm
    return pl.pallas_call(
        functools.partial(_moe_combine_kernel, alpha=alpha),
        grid=(nb,),
        in_specs=[pl.BlockSpec((bm, d), lambda i: (i, 0)),
                  pl.BlockSpec((bm, d), lambda i: (i + nb, 0)),
                  pl.BlockSpec((bm, LANES), lambda i: (i, 0)),
                  pl.BlockSpec((bm, d), lambda i: (i, 0)),
                  pl.BlockSpec((1, d), lambda i: (0, 0)),
                  pl.BlockSpec((1, d), lambda i: (0, 0))],
        out_specs=[pl.BlockSpec((bm, d), lambda i: (i, 0)),
                   pl.BlockSpec((bm, d), lambda i: (i, 0))],
        out_shape=[jax.ShapeDtypeStruct((n, d), F32), jax.ShapeDtypeStruct((n, d), BF16)],
        compiler_params=_params(("parallel",)),
        name="moe_combine",
    )(y_both, y_both, route, x, g.reshape(1, d), b.reshape(1, d))


def _route_plan(route, n_experts, bm):
    n = route.shape[0]
    n_assign = TOP_K_EXPERTS * n
    n_rows = n_assign + n_experts * bm
    flat_e = route[:, :TOP_K_EXPERTS].astype(I32).reshape(n_assign)
    onehot = (flat_e[:, None] == jnp.arange(n_experts, dtype=I32)[None, :]).astype(I32)
    csum = jnp.cumsum(onehot, axis=0)
    rank = jnp.sum(onehot * csum, axis=1) - 1
    counts = csum[-1]
    padded = (counts + bm - 1) // bm * bm
    ends = jnp.cumsum(padded)
    starts = ends - padded
    dest = jnp.sum(onehot * starts[None, :], axis=1) + rank
    src = jnp.zeros((n_rows,), I32).at[dest].set(jnp.arange(n_assign, dtype=I32) // TOP_K_EXPERTS)
    tile_start = jnp.arange(n_rows // bm, dtype=I32) * bm
    tile_expert = jnp.minimum(jnp.sum((tile_start[:, None] >= ends[None, :]).astype(I32), axis=1),
                              n_experts - 1)
    tile_valid = (tile_start < ends[-1]).astype(I32)
    return src, dest, tile_expert, tile_valid


def _rope_tables(positions):
    pos = positions.astype(F32)[..., None]
    inv_h = ROPE_THETA ** (-jnp.arange(0, HEAD_DIM, 2, dtype=F32) / HEAD_DIM)
    inv_i = ROPE_THETA ** (-jnp.arange(0, IDX_DIM, 2, dtype=F32) / IDX_DIM)
    ang_h = pos * inv_h
    ang_i = pos * inv_i
    ch, sh = jnp.cos(ang_h), jnp.sin(ang_h)
    ci, si = jnp.cos(ang_i), jnp.sin(ang_i)
    z32 = jnp.zeros_like(si)
    one64 = jnp.ones(ci.shape[:-1] + (IDX_DIM,), F32)
    z64 = jnp.zeros_like(one64)
    cat = lambda *a: jnp.concatenate(a, axis=-1).reshape(-1, LANES)
    main = (cat(ch, ch), cat(-sh, sh), cat(ci, ci, ci, ci), cat(-si, z32, -si, z32),
            cat(z32, si, z32, si))
    small = (main[0], main[1], cat(ci, ci, one64), cat(-si, z32, z64), cat(z32, si, z64))
    return main, small


def kernel(x, positions, ln_in_g, ln_in_b, w_in, w_proj_a, w_proj_b, w_out, ln_mix_g, ln_mix_b,
           ffn_w_gate, ffn_w_up, ffn_w_down, router_w, moe_w_gate, moe_w_up, moe_w_down,
           ln_ffn_g, ln_ffn_b):
    bsz, s_len, d = x.shape
    depth = w_in.shape[0]
    n = bsz * s_len
    n_heads = d // (2 * HEAD_DIM)
    d_a = n_heads * HEAD_DIM
    d_qi = IDX_HEADS * IDX_DIM
    n_experts = router_w.shape[-1]
    alpha = (2.0 * depth) ** 0.25
    n_sel = min(TOPK_MAX, s_len // 4)
    tq = 256
    assert s_len % tq == 0 and tq % CHUNK == 0 and n_sel <= tq and d_qi % d_a == 0

    off_ki = 3 * d_a + d_qi
    off_qb = off_ki + IDX_DIM + IDX_HEADS
    qb_col_block = (3 * d_a + d_qi) // d_a
    ga_col = 6 * d_a + d_qi

    tabs_main, tabs_small = _rope_tables(positions)
    tri = (lax.broadcasted_iota(I32, (tq, tq), 0) >= lax.broadcasted_iota(I32, (tq, tq), 1)
           ).astype(BF16)

    xf, xb = _input_ln(x.reshape(n, d), ln_in_g, ln_in_b)
    for layer in range(depth):
        w_l = w_in[layer]
        w_main = jnp.concatenate([w_l[:, :off_ki], w_l[:, off_qb:]], axis=1).astype(BF16)
        w_small = jnp.pad(w_l[:, off_ki:off_qb],
                          ((0, 0), (0, LANES - IDX_DIM - IDX_HEADS))).astype(BF16)
        hmain = _inproj(xb, w_main, tabs_main, bn=d_a, rope128_tiles=2,
                        rope64_tiles=(3, 3 + d_qi // d_a), out_dtype=BF16)
        hsmall = _inproj(xb, w_small, tabs_small, bn=LANES, rope128_tiles=0,
                         rope64_tiles=(0, 1), out_dtype=F32)
        h3 = hmain.reshape(bsz, s_len, -1)
        hs3 = hsmall.reshape(bsz, s_len, LANES)

        qat = h3[..., :d_a].reshape(bsz, s_len, n_heads, HEAD_DIM).transpose(0, 2, 3, 1)
        vat = (h3[..., 2 * d_a:3 * d_a].reshape(bsz, s_len // tq, tq, n_heads, HEAD_DIM)
               .transpose(0, 3, 1, 4, 2))
        qit = (h3[..., 3 * d_a:3 * d_a + d_qi].reshape(bsz, s_len, IDX_HEADS, IDX_DIM)
               .transpose(0, 2, 3, 1))
        ki = hs3[..., :IDX_DIM].astype(BF16)
        wit = hs3[..., IDX_DIM:IDX_DIM + IDX_HEADS].transpose(0, 2, 1)

        bias = _indexer(qit, ki, wit, tq=tq, n_sel=n_sel)
        oat = _sparse_attn(qat, h3, vat, bias, tq=tq, n_heads=n_heads, k_col_block=1)
        o_a = oat.transpose(0, 3, 1, 2).reshape(n, d_a)
        o_b = _stick_attn(h3, tri, tq=tq, n_heads=n_heads, q_col_block=qb_col_block
                          ).reshape(n, d_a)

        merged = _gated_proj(o_a, o_b, w_proj_a[layer].astype(BF16), w_proj_b[layer].astype(BF16),
                             hmain, ga_col=ga_col)
        xf, xb = _mm_res_ln(merged, w_out[layer].astype(BF16)[None], xf, ln_mix_g[layer],
                            ln_mix_b[layer], alpha=alpha, bk=_pick(d, 1024))

        i = layer // 2
        if layer % 2 == 0:
            hid = _swiglu_hidden(xb, ffn_w_gate[i].astype(BF16), ffn_w_up[i].astype(BF16))
            f_dim = hid.shape[1]
            bk = max(b for b in (1408, 1024, 512, 256, 128) if f_dim % b == 0)
            xf, xb = _mm_res_ln(hid, ffn_w_down[i].astype(BF16)[None], xf, ln_ffn_g[layer],
                                ln_ffn_b[layer], alpha=alpha, bk=bk)
        else:
            rw = jnp.pad(router_w[i], ((0, 0), (0, LANES - n_experts))).astype(BF16)
            route = _router(xb, rw, n_experts=n_experts)
            bm_e = _pick(n, 512)
            src, dest, tile_expert, tile_valid = _route_plan(route, n_experts, bm_e)
            xs = _row_gather(xf, src)
            hs = _moe_up(xs, moe_w_gate[i].astype(BF16), moe_w_up[i].astype(BF16),
                         tile_expert, tile_valid, bm=bm_e)
            ys = _moe_down(hs, moe_w_down[i].astype(BF16), tile_expert, tile_valid, bm=bm_e)
            y_both = _row_gather(ys, dest.reshape(n, TOP_K_EXPERTS).T.reshape(-1))
            xf, xb = _moe_combine(y_both, route, xf, ln_ffn_g[layer], ln_ffn_b[layer],
                                  alpha=alpha)
    return xf.reshape(bsz, s_len, d)
```

```python
import functools
import math

import jax
import jax.numpy as jnp
from jax import lax
from jax.experimental import pallas as pl
from jax.experimental.pallas import tpu as pltpu

F32 = jnp.float32
BF16 = jnp.bfloat16
I32 = jnp.int32

LN_EPS = 1e-5
HEAD_DIM = 128
IDX_HEADS = 16
IDX_DIM = 64
CHUNK = 64
TOPK_MAX = 256
ROPE_THETA = 10000.0
TOP_K_EXPERTS = 2
LANES = 128
VMEM_LIMIT = 56 * 1024 * 1024
MASK_NEG = -1e30
KEY_NEG_INF = -2139095041
STICK_SKIP_LOG = -100.0
SPARSE_Q_SCALE = math.log2(math.e) / math.sqrt(HEAD_DIM)
ONES_ROWS = 16
INT_MIN = -2147483648


def _params(sem):
    return pltpu.CompilerParams(dimension_semantics=sem, vmem_limit_bytes=VMEM_LIMIT)


def _pick(n, pref):
    b = min(n, pref)
    while n % b:
        b //= 2
    return b


def _layer_norm(v, g, b):
    mu = jnp.mean(v, axis=-1, keepdims=True)
    c = v - mu
    var = jnp.mean(c * c, axis=-1, keepdims=True)
    return c * lax.rsqrt(var + LN_EPS) * g + b


def _ln_kernel(x_ref, g_ref, b_ref, of_ref, ob_ref):
    y = _layer_norm(x_ref[...], g_ref[...], b_ref[...])
    of_ref[...] = y
    ob_ref[...] = y.astype(BF16)


def _input_ln(x2d, g, b):
    n, d = x2d.shape
    bm = _pick(n, 512)
    return pl.pallas_call(
        _ln_kernel,
        grid=(n // bm,),
        in_specs=[pl.BlockSpec((bm, d), lambda i: (i, 0)),
                  pl.BlockSpec((1, d), lambda i: (0, 0)),
                  pl.BlockSpec((1, d), lambda i: (0, 0))],
        out_specs=[pl.BlockSpec((bm, d), lambda i: (i, 0)),
                   pl.BlockSpec((bm, d), lambda i: (i, 0))],
        out_shape=[jax.ShapeDtypeStruct((n, d), F32), jax.ShapeDtypeStruct((n, d), BF16)],
        compiler_params=_params(("parallel",)),
        name="input_ln",
    )(x2d, g.reshape(1, d), b.reshape(1, d))


def _inproj_kernel(x_ref, w_ref, c128_ref, s128_ref, c64_ref, sa64_ref, sb64_ref, o_ref, *,
                   rope128_tiles, rope64_tiles):
    j = pl.program_id(1)
    n_heads = o_ref.shape[1] // LANES
    lo64, hi64 = rope64_tiles
    is128 = j < rope128_tiles
    is64 = jnp.logical_and(j >= lo64, j < hi64)

    def project():
        return jnp.dot(x_ref[...], w_ref[...], preferred_element_type=F32)

    @pl.when(is128)
    def _():
        q_scale = jnp.where(j == 0, SPARSE_Q_SCALE, 1.0)
        c = c128_ref[...] * q_scale
        s = s128_ref[...] * q_scale
        y = project()
        for h in range(n_heads):
            yh = y[:, h * LANES:(h + 1) * LANES]
            o_ref[:, h * LANES:(h + 1) * LANES] = (
                yh * c + pltpu.roll(yh, HEAD_DIM // 2, 1) * s).astype(o_ref.dtype)

    @pl.when(is64)
    def _():
        c = c64_ref[...]
        sa = sa64_ref[...]
        sb = sb64_ref[...]
        y = project()
        for h in range(n_heads):
            yh = y[:, h * LANES:(h + 1) * LANES]
            o_ref[:, h * LANES:(h + 1) * LANES] = (
                yh * c + pltpu.roll(yh, LANES - IDX_DIM // 2, 1) * sa
                + pltpu.roll(yh, IDX_DIM // 2, 1) * sb).astype(o_ref.dtype)

    @pl.when(jnp.logical_not(jnp.logical_or(is128, is64)))
    def _():
        o_ref[...] = project().astype(o_ref.dtype)


def _inproj(xb, w, tabs, *, bn, rope128_tiles, rope64_tiles, out_dtype):
    n, d = xb.shape
    cols = w.shape[1]
    bm = _pick(n, 1024)
    tab_spec = pl.BlockSpec((bm, LANES), lambda i, j: (i, 0))
    return pl.pallas_call(
        functools.partial(_inproj_kernel, rope128_tiles=rope128_tiles, rope64_tiles=rope64_tiles),
        grid=(n // bm, cols // bn),
        in_specs=[pl.BlockSpec((bm, d), lambda i, j: (i, 0)),
                  pl.BlockSpec((d, bn), lambda i, j: (0, j))] + [tab_spec] * 5,
        out_specs=pl.BlockSpec((bm, bn), lambda i, j: (i, j)),
        out_shape=jax.ShapeDtypeStruct((n, cols), out_dtype),
        compiler_params=_params(("parallel", "arbitrary")),
        name="inproj",
    )(xb, w, *tabs)


def _indexer_kernel(qit_ref, ki_ref, wit_ref, tri_ref, bias_ref, skey_ref, *, tq, n_sel):
    qb = pl.program_id(1)
    n_kt = bias_ref.shape[0] // tq
    q_chunk = (qb * tq + lax.broadcasted_iota(I32, (tq, tq), 1)) // CHUNK
    k_local = lax.broadcasted_iota(I32, (tq, tq), 0)

    def score_tile(kt, carry):
        k0 = pl.multiple_of(kt * tq, tq)
        k_t = ki_ref[pl.ds(k0, tq), :]
        acc = jnp.zeros((tq, tq), F32)
        for h in range(IDX_HEADS):
            lg = jnp.dot(k_t, qit_ref[h], preferred_element_type=F32)
            acc = acc + jnp.maximum(lg, 0.0) * wit_ref[h:h + 1, :]
        acc = jnp.where((k0 + k_local) // CHUNK <= q_chunk, acc, -jnp.inf)
        bits = pltpu.bitcast(acc, I32)
        skey_ref[pl.ds(k0, tq), :] = bits ^ ((bits >> 31) & 0x7FFFFFFF)
        return carry

    lax.fori_loop(0, qb + 1, score_tile, 0)

    def count_ge(cand):
        def body(kt, accs):
            k0 = pl.multiple_of(kt * tq, tq)
            accs = list(accs)
            x = skey_ref[pl.ds(k0, tq), :]
            for r in range(tq // 8):
                m = x[r * 8:(r + 1) * 8, :] >= cand
                a = accs[r % len(accs)]
                accs[r % len(accs)] = jnp.where(m, a + 1, a)
            return tuple(accs)
        accs = lax.fori_loop(0, qb + 1, body, tuple(jnp.zeros((8, tq), I32) for _ in range(4)))
        return (accs[0] + accs[1] + accs[2] + accs[3]).sum(axis=0, keepdims=True)

    def bit_step(i, carry):
        ans, cnt = carry
        cand = ans + lax.shift_left(jnp.int32(1), 31 - i)
        cnt_cand = count_ge(cand)
        keep = cnt_cand >= n_sel
        return jnp.where(keep, cand, ans), jnp.where(keep, cnt_cand, cnt)

    n_bits = jnp.where((qb + 1) * tq > n_sel, 32, 0)
    thr, cnt = lax.fori_loop(0, n_bits, bit_step, (jnp.full((1, tq), INT_MIN, I32),
                                                   jnp.zeros((1, tq), I32)))
    tied = jnp.where(thr > KEY_NEG_INF, jnp.where(cnt > n_sel, 1, 0), 0)
    any_tied = jnp.max(tied) > 0
    thr = jnp.maximum(thr, KEY_NEG_INF + 1)

    @pl.when(jnp.logical_not(any_tied))
    def _():
        def bias_tile(kt, carry):
            k0 = pl.multiple_of(kt * tq, tq)
            sel = skey_ref[pl.ds(k0, tq), :] >= thr
            bias_ref[pl.ds(k0, tq), :] = jnp.where(sel, 0.0, MASK_NEG).astype(BF16)
            return carry

        lax.fori_loop(0, qb + 1, bias_tile, 0)

    @pl.when(any_tied)
    def _():
        need = (n_sel - count_ge(thr + 1)).astype(F32)
        tri = tri_ref[...]

        def bias_tile(kt, seen):
            k0 = pl.multiple_of(kt * tq, tq)
            x = skey_ref[pl.ds(k0, tq), :]
            eq = x == thr
            rank = seen + jnp.dot(tri, jnp.where(eq, 1.0, 0.0).astype(BF16),
                                  preferred_element_type=F32)
            tie_bias = jnp.where(eq, jnp.where(rank <= need, 0.0, MASK_NEG), MASK_NEG)
            bias_ref[pl.ds(k0, tq), :] = jnp.where(x > thr, 0.0, tie_bias).astype(BF16)
            return rank[tq - 1:tq, :]

        lax.fori_loop(0, qb + 1, bias_tile, jnp.zeros((1, tq), F32))

    def fill_tile(kt, carry):
        k0 = pl.multiple_of(kt * tq, tq)
        bias_ref[pl.ds(k0, tq), :] = jnp.full((tq, tq), MASK_NEG, BF16)
        return carry

    lax.fori_loop(qb + 1, n_kt, fill_tile, 0)


def _indexer(qit, ki, wit, tri, *, tq, n_sel):
    bsz, _, _, s_len = qit.shape
    return pl.pallas_call(
        functools.partial(_indexer_kernel, tq=tq, n_sel=n_sel),
        grid=(bsz, s_len // tq),
        in_specs=[pl.BlockSpec((None, IDX_HEADS, IDX_DIM, tq), lambda b, q: (b, 0, 0, q)),
                  pl.BlockSpec((None, s_len, IDX_DIM), lambda b, q: (b, 0, 0)),
                  pl.BlockSpec((None, IDX_HEADS, tq), lambda b, q: (b, 0, q)),
                  pl.BlockSpec((tq, tq), lambda b, q: (0, 0))],
        out_specs=pl.BlockSpec((None, s_len, tq), lambda b, q: (b, 0, q)),
        out_shape=jax.ShapeDtypeStruct((bsz, s_len, s_len), BF16),
        scratch_shapes=[pltpu.VMEM((s_len, tq), I32)],
        compiler_params=_params(("parallel", "arbitrary")),
        name="indexer_topk",
    )(qit, ki, wit, tri)


def _sparse_attn_kernel(qt_ref, k_ref, vt_ref, bias_ref, ot_ref, acc_ref, s_ref, p_ref, *, tq,
                        n_heads):
    qb = pl.program_id(1)
    acc_ref[...] = jnp.zeros(acc_ref.shape, F32)

    def body(kt, ms):
        k0 = pl.multiple_of(kt * tq, tq)
        for h in range(n_heads):
            k_t = k_ref[pl.ds(k0, tq), h * HEAD_DIM:(h + 1) * HEAD_DIM]
            s_ref[h] = jnp.dot(k_t, qt_ref[h], preferred_element_type=F32)
        bias = bias_ref[pl.ds(k0, tq), :].astype(F32)
        new_ms, alphas = [], []
        for h in range(n_heads):
            s = s_ref[h] + bias
            m_new = jnp.maximum(ms[h], s.max(axis=0, keepdims=True))
            alphas.append(jnp.exp2(ms[h] - m_new))
            p_ref[h] = jnp.exp2(s - m_new).astype(BF16)
            new_ms.append(m_new)
        for h in range(n_heads):
            acc_ref[h] = acc_ref[h] * alphas[h] + jnp.dot(vt_ref[h, kt], p_ref[h],
                                                          preferred_element_type=F32)
        return tuple(new_ms)

    lax.fori_loop(0, qb + 1, body, tuple(jnp.full((1, tq), -jnp.inf, F32)
                                         for _ in range(n_heads)))
    for h in range(n_heads):
        ot_ref[h] = (acc_ref[h, :HEAD_DIM, :] / acc_ref[h, HEAD_DIM:HEAD_DIM + 1, :]
                     ).astype(ot_ref.dtype)


def _sparse_attn(qat, h3, vat, bias, *, tq, n_heads, k_col_block):
    bsz, s_len, _ = h3.shape
    d_a = n_heads * HEAD_DIM
    n_kt = s_len // tq
    v_rows = HEAD_DIM + ONES_ROWS
    assert vat.shape == (bsz, n_heads, n_kt, v_rows, tq)
    return pl.pallas_call(
        functools.partial(_sparse_attn_kernel, tq=tq, n_heads=n_heads),
        grid=(bsz, s_len // tq),
        in_specs=[pl.BlockSpec((None, n_heads, HEAD_DIM, tq), lambda b, q: (b, 0, 0, q)),
                  pl.BlockSpec((None, s_len, d_a), lambda b, q: (b, 0, k_col_block)),
                  pl.BlockSpec((None, n_heads, n_kt, v_rows, tq), lambda b, q: (b, 0, 0, 0, 0)),
                  pl.BlockSpec((None, s_len, tq), lambda b, q: (b, 0, q))],
        out_specs=pl.BlockSpec((None, n_heads, HEAD_DIM, tq), lambda b, q: (b, 0, 0, q)),
        out_shape=jax.ShapeDtypeStruct((bsz, n_heads, HEAD_DIM, s_len), BF16),
        scratch_shapes=[pltpu.VMEM((n_heads, v_rows, tq), F32),
                        pltpu.VMEM((n_heads, tq, tq), F32),
                        pltpu.VMEM((n_heads, tq, tq), BF16)],
        compiler_params=_params(("parallel", "arbitrary")),
        name="sparse_attn",
    )(qat, h3, vat, bias)


def _stick_kernel(q_ref, k_ref, v_ref, tri_ref, o_ref, kmax_ref, acc_ref, run_ref, z_ref, suf_ref,
                  hi_ref, lo_ref, a_ref, *, tq, n_heads):
    qb = pl.program_id(1)
    s_len, d_b = k_ref.shape
    scale = 1.0 / math.sqrt(HEAD_DIM)
    heads = [slice(h * HEAD_DIM, (h + 1) * HEAD_DIM) for h in range(n_heads)]

    @pl.when(qb == 0)
    def _():
        def body(kt, m8):
            k_t = jnp.abs(k_ref[pl.ds(pl.multiple_of(kt * tq, tq), tq), :].astype(F32))
            return jnp.maximum(m8, k_t.reshape(tq // 8, 8, d_b).max(axis=0))
        m8 = lax.fori_loop(0, s_len // tq, body, jnp.zeros((8, d_b), F32))
        kmax_ref[...] = m8.max(axis=0, keepdims=True)

    z_bounds = [jnp.sum(jnp.abs(q_ref[:, c].astype(F32)) * kmax_ref[:, c], axis=1, keepdims=True)
                * scale for c in heads]
    tri = tri_ref[...]

    def tile_step(k0, strict):
        for h, c in enumerate(heads):
            z_ref[h] = lax.dot_general(q_ref[:, c], k_ref[pl.ds(k0, tq), c],
                                       (((1,), (1,)), ((), ())), preferred_element_type=F32)
        for h in range(n_heads):
            z = z_ref[h] * scale
            log_keep = -(jnp.maximum(z, 0.0) + jnp.log(1.0 + jnp.exp(-jnp.abs(z))))
            if strict is not None:
                log_keep = jnp.where(strict, log_keep, 0.0)
            hi = log_keep.astype(BF16)
            hi_ref[h] = hi
            lo_ref[h] = (log_keep - hi.astype(F32)).astype(BF16)
        for h in range(n_heads):
            suf_ref[h] = (jnp.dot(hi_ref[h], tri, preferred_element_type=F32)
                          + jnp.dot(lo_ref[h], tri, preferred_element_type=F32))
        bound = None
        for h in range(n_heads):
            suffix = suf_ref[h]
            a = jnp.exp(z_ref[h] * scale + suffix + run_ref[h])
            if strict is not None:
                a = jnp.where(strict, a, 0.0)
            a_ref[h] = a.astype(BF16)
            run = run_ref[h] + suffix[:, 0:1]
            run_ref[h] = run
            b_h = jnp.max(run + z_bounds[h])
            bound = b_h if bound is None else jnp.maximum(bound, b_h)
        for h, c in enumerate(heads):
            acc_ref[:, c] += jnp.dot(a_ref[h], v_ref[pl.ds(k0, tq), c],
                                     preferred_element_type=F32)
        return bound

    acc_ref[...] = jnp.zeros(acc_ref.shape, F32)
    run_ref[...] = jnp.zeros(run_ref.shape, F32)
    strict = lax.broadcasted_iota(I32, (tq, tq), 1) < lax.broadcasted_iota(I32, (tq, tq), 0)
    bound0 = tile_step(pl.multiple_of(qb * tq, tq), strict)

    def cond(carry):
        i, bound = carry
        return jnp.logical_and(i <= qb, bound > STICK_SKIP_LOG)

    def body(carry):
        i, _ = carry
        return i + 1, tile_step(pl.multiple_of((qb - i) * tq, tq), None)

    lax.while_loop(cond, body, (jnp.int32(1), bound0))
    o_ref[...] = acc_ref[...].astype(o_ref.dtype)


def _stick_attn(h3, tri, *, tq, n_heads, q_col_block):
    bsz, s_len, _ = h3.shape
    d_b = n_heads * HEAD_DIM
    return pl.pallas_call(
        functools.partial(_stick_kernel, tq=tq, n_heads=n_heads),
        grid=(bsz, s_len // tq),
        in_specs=[pl.BlockSpec((None, tq, d_b), lambda b, q: (b, q, q_col_block)),
                  pl.BlockSpec((None, s_len, d_b), lambda b, q: (b, 0, q_col_block + 1)),
                  pl.BlockSpec((None, s_len, d_b), lambda b, q: (b, 0, q_col_block + 2)),
                  pl.BlockSpec((tq, tq), lambda b, q: (0, 0))],
        out_specs=pl.BlockSpec((None, tq, d_b), lambda b, q: (b, q, 0)),
        out_shape=jax.ShapeDtypeStruct((bsz, s_len, d_b), BF16),
        scratch_shapes=[pltpu.VMEM((1, d_b), F32),
                        pltpu.VMEM((tq, d_b), F32),
                        pltpu.VMEM((n_heads, tq, 1), F32),
                        pltpu.VMEM((n_heads, tq, tq), F32),
                        pltpu.VMEM((n_heads, tq, tq), F32),
                        pltpu.VMEM((n_heads, tq, tq), BF16),
                        pltpu.VMEM((n_heads, tq, tq), BF16),
                        pltpu.VMEM((n_heads, tq, tq), BF16)],
        compiler_params=_params(("arbitrary", "arbitrary")),
        name="stick_attn",
    )(h3, h3, h3, tri)


def _gated_proj_kernel(oa_ref, ob_ref, pa_ref, pb_ref, ga_ref, gb_ref, o_ref):
    ya = jnp.dot(oa_ref[...], pa_ref[...], preferred_element_type=F32)
    yb = jnp.dot(ob_ref[...], pb_ref[...], preferred_element_type=F32)
    sa = 1.0 / (1.0 + jnp.exp(-ga_ref[...].astype(F32)))
    sb = 1.0 / (1.0 + jnp.exp(-gb_ref[...].astype(F32)))
    o_ref[...] = (sa * ya + sb * yb).astype(o_ref.dtype)


def _gated_proj(oa, ob, pa, pb, hmain, *, ga_col):
    n, d_a = oa.shape
    d = pa.shape[1]
    bm = _pick(n, 1024)
    bn = _pick(d, 1024)
    ga_blk = ga_col // bn
    gb_blk = (ga_col + d) // bn
    return pl.pallas_call(
        _gated_proj_kernel,
        grid=(n // bm, d // bn),
        in_specs=[pl.BlockSpec((bm, d_a), lambda i, j: (i, 0)),
                  pl.BlockSpec((bm, d_a), lambda i, j: (i, 0)),
                  pl.BlockSpec((d_a, bn), lambda i, j: (0, j)),
                  pl.BlockSpec((d_a, bn), lambda i, j: (0, j)),
                  pl.BlockSpec((bm, bn), lambda i, j: (i, ga_blk + j)),
                  pl.BlockSpec((bm, bn), lambda i, j: (i, gb_blk + j))],
        out_specs=pl.BlockSpec((bm, bn), lambda i, j: (i, j)),
        out_shape=jax.ShapeDtypeStruct((n, d), BF16),
        compiler_params=_params(("parallel", "arbitrary")),
        name="gated_proj",
    )(oa, ob, pa, pb, hmain, hmain)


def _mm_res_ln_kernel(a_ref, w_ref, x_ref, g_ref, b_ref, of_ref, ob_ref, *, alpha):
    k = pl.program_id(1)

    @pl.when(k == 0)
    def _():
        of_ref[...] = jnp.dot(a_ref[...], w_ref[...], preferred_element_type=F32)

    @pl.when(k > 0)
    def _():
        of_ref[...] += jnp.dot(a_ref[...], w_ref[...], preferred_element_type=F32)

    @pl.when(k == pl.num_programs(1) - 1)
    def _():
        y = _layer_norm(alpha * x_ref[...] + of_ref[...], g_ref[...], b_ref[...])
        of_ref[...] = y
        ob_ref[...] = y.astype(BF16)


def _mm_res_ln(a, w, x, g, b, *, alpha, bm, bk):
    n, kk = a.shape
    d = w.shape[1]
    assert n % bm == 0 and kk % bk == 0
    return pl.pallas_call(
        functools.partial(_mm_res_ln_kernel, alpha=alpha),
        grid=(n // bm, kk // bk),
        in_specs=[pl.BlockSpec((bm, bk), lambda i, k: (i, k)),
                  pl.BlockSpec((bk, d), lambda i, k: (k, 0)),
                  pl.BlockSpec((bm, d), lambda i, k: (i, 0)),
                  pl.BlockSpec((1, d), lambda i, k: (0, 0)),
                  pl.BlockSpec((1, d), lambda i, k: (0, 0))],
        out_specs=[pl.BlockSpec((bm, d), lambda i, k: (i, 0)),
                   pl.BlockSpec((bm, d), lambda i, k: (i, 0))],
        out_shape=[jax.ShapeDtypeStruct((n, d), F32), jax.ShapeDtypeStruct((n, d), BF16)],
        compiler_params=_params(("parallel", "arbitrary")),
        name="mm_res_ln",
    )(a, w, x, g.reshape(1, d), b.reshape(1, d))


def _swiglu(x, wg, wu):
    g = jnp.dot(x, wg, preferred_element_type=F32)
    u = jnp.dot(x, wu, preferred_element_type=F32)
    return g / (1.0 + jnp.exp(-g)) * u


def _swiglu_kernel(x_ref, wg_ref, wu_ref, o_ref):
    o_ref[...] = _swiglu(x_ref[...], wg_ref[...], wu_ref[...]).astype(o_ref.dtype)


def _swiglu_hidden(xb, wg, wu):
    n, d = xb.shape
    f = wg.shape[1]
    bm = _pick(n, 1024)
    bn = _pick(f, 512)
    return pl.pallas_call(
        _swiglu_kernel,
        grid=(n // bm, f // bn),
        in_specs=[pl.BlockSpec((bm, d), lambda i, j: (i, 0)),
                  pl.BlockSpec((d, bn), lambda i, j: (0, j)),
                  pl.BlockSpec((d, bn), lambda i, j: (0, j))],
        out_specs=pl.BlockSpec((bm, bn), lambda i, j: (i, j)),
        out_shape=jax.ShapeDtypeStruct((n, f), BF16),
        compiler_params=_params(("parallel", "arbitrary")),
        name="swiglu_hidden",
    )(xb, wg, wu)


def _router_kernel(x_ref, w_ref, comb_ref, *, n_experts):
    logits = jnp.dot(x_ref[...], w_ref[...], preferred_element_type=F32)
    lane = lax.broadcasted_iota(I32, logits.shape, 1).astype(F32)
    logits = jnp.where(lane < n_experts, logits, -jnp.inf)
    v1 = jnp.max(logits, axis=1, keepdims=True)
    i1 = jnp.min(jnp.where(logits == v1, lane, float(LANES)), axis=1, keepdims=True)
    rest = jnp.where(lane == i1, -jnp.inf, logits)
    v2 = jnp.max(rest, axis=1, keepdims=True)
    i2 = jnp.min(jnp.where(rest == v2, lane, float(LANES)), axis=1, keepdims=True)
    e2 = jnp.exp(v2 - v1)
    g1 = 1.0 / (1.0 + e2)
    g2 = e2 / (1.0 + e2)
    comb_ref[...] = jnp.where(lane == 0.0, i1, jnp.where(lane == 1.0, i2, jnp.where(
        lane == 2.0, g1, jnp.where(lane == 3.0, g2, 0.0))))


def _router(xb, rw, *, n_experts):
    n, d = xb.shape
    bm = _pick(n, 1024)
    return pl.pallas_call(
        functools.partial(_router_kernel, n_experts=n_experts),
        grid=(n // bm,),
        in_specs=[pl.BlockSpec((bm, d), lambda i: (i, 0)),
                  pl.BlockSpec((d, LANES), lambda i: (0, 0))],
        out_specs=pl.BlockSpec((bm, LANES), lambda i: (i, 0)),
        out_shape=jax.ShapeDtypeStruct((n, LANES), F32),
        compiler_params=_params(("parallel",)),
        name="router",
    )(xb, rw)


def _row_gather_kernel(idx_ref, src_ref, dst_ref, sem, *, rows, window):

    def row_copy(r):
        return pltpu.make_async_copy(src_ref.at[pl.ds(idx_ref[r], 1)],
                                     dst_ref.at[pl.ds(r, 1)], sem)

    def prime(r, carry):
        row_copy(r).start()
        return carry

    def steady(r, carry):
        row_copy(r - window).wait()
        row_copy(r).start()
        return carry

    def drain(r, carry):
        row_copy(r).wait()
        return carry

    lax.fori_loop(0, window, prime, 0, unroll=8)
    lax.fori_loop(window, rows, steady, 0, unroll=8)
    lax.fori_loop(rows - window, rows, drain, 0, unroll=8)


def _row_gather(src, idx):
    n_out = idx.shape[0]
    rows = _pick(n_out, 512)
    window = 256
    assert rows >= window and rows % 8 == 0 and window % 8 == 0
    return pl.pallas_call(
        functools.partial(_row_gather_kernel, rows=rows, window=window),
        grid=(n_out // rows,),
        in_specs=[pl.BlockSpec((rows,), lambda i: (i,), memory_space=pltpu.SMEM),
                  pl.BlockSpec(memory_space=pl.ANY)],
        out_specs=pl.BlockSpec((rows,) + src.shape[1:], lambda i: (i, 0)),
        out_shape=jax.ShapeDtypeStruct((n_out,) + src.shape[1:], src.dtype),
        scratch_shapes=[pltpu.SemaphoreType.DMA(())],
        compiler_params=_params(("arbitrary",)),
        name="row_gather",
    )(idx, src)


def _moe_up_kernel(te_ref, tv_ref, x_ref, wg_ref, wu_ref, o_ref):
    t = pl.program_id(1)

    @pl.when(tv_ref[t] != 0)
    def _():
        o_ref[...] = _swiglu(x_ref[...].astype(BF16), wg_ref[...], wu_ref[...]).astype(o_ref.dtype)

    @pl.when(tv_ref[t] == 0)
    def _():
        o_ref[...] = jnp.zeros(o_ref.shape, o_ref.dtype)


def _moe_up(xs, wg, wu, tile_expert, tile_valid, *, bm):
    n_rows, d = xs.shape
    f = wg.shape[2]
    bn = _pick(f, 1024)
    grid_spec = pltpu.PrefetchScalarGridSpec(
        num_scalar_prefetch=2,
        grid=(f // bn, n_rows // bm),
        in_specs=[pl.BlockSpec((bm, d), lambda j, t, te, tv: (t, 0)),
                  pl.BlockSpec((None, d, bn), lambda j, t, te, tv: (te[t], 0, j)),
                  pl.BlockSpec((None, d, bn), lambda j, t, te, tv: (te[t], 0, j))],
        out_specs=pl.BlockSpec((bm, bn), lambda j, t, te, tv: (t, j)))
    return pl.pallas_call(
        _moe_up_kernel,
        grid_spec=grid_spec,
        out_shape=jax.ShapeDtypeStruct((n_rows, f), BF16),
        compiler_params=_params(("arbitrary", "arbitrary")),
        name="moe_up",
    )(tile_expert, tile_valid, xs, wg, wu)


def _moe_down_kernel(te_ref, tv_ref, h_ref, w_ref, o_ref):
    t = pl.program_id(0)

    @pl.when(pl.program_id(1) == 0)
    def _():
        o_ref[...] = jnp.zeros(o_ref.shape, o_ref.dtype)

    @pl.when(tv_ref[t] != 0)
    def _():
        o_ref[...] += jnp.dot(h_ref[...], w_ref[...], preferred_element_type=F32)


def _moe_down(hs, wd, tile_expert, tile_valid, *, bm):
    n_rows, f = hs.shape
    d = wd.shape[2]
    bk = _pick(f, 1024)
    grid_spec = pltpu.PrefetchScalarGridSpec(
        num_scalar_prefetch=2,
        grid=(n_rows // bm, f // bk),
        in_specs=[pl.BlockSpec((bm, bk), lambda t, k, te, tv: (t, k)),
                  pl.BlockSpec((None, bk, d), lambda t, k, te, tv: (te[t], k, 0))],
        out_specs=pl.BlockSpec((bm, d), lambda t, k, te, tv: (t, 0)))
    return pl.pallas_call(
        _moe_down_kernel,
        grid_spec=grid_spec,
        out_shape=jax.ShapeDtypeStruct((n_rows, d), F32),
        compiler_params=_params(("arbitrary", "arbitrary")),
        name="moe_down",
    )(tile_expert, tile_valid, hs, wd)


def _moe_combine_kernel(y1_ref, y2_ref, r_ref, x_ref, g_ref, b_ref, of_ref, ob_ref, *, alpha):
    route = r_ref[...]
    y = route[:, 2:3] * y1_ref[...] + route[:, 3:4] * y2_ref[...]
    out = _layer_norm(alpha * x_ref[...] + y, g_ref[...], b_ref[...])
    of_ref[...] = out
    ob_ref[...] = out.astype(BF16)


def _moe_combine(y_both, route, x, g, b, *, alpha):
    n, d = x.shape
    bm = _pick(n, 256)
    nb = n // bm
    return pl.pallas_call(
        functools.partial(_moe_combine_kernel, alpha=alpha),
        grid=(nb,),
        in_specs=[pl.BlockSpec((bm, d), lambda i: (i, 0)),
                  pl.BlockSpec((bm, d), lambda i: (i + nb, 0)),
                  pl.BlockSpec((bm, LANES), lambda i: (i, 0)),
                  pl.BlockSpec((bm, d), lambda i: (i, 0)),
                  pl.BlockSpec((1, d), lambda i: (0, 0)),
                  pl.BlockSpec((1, d), lambda i: (0, 0))],
        out_specs=[pl.BlockSpec((bm, d), lambda i: (i, 0)),
                   pl.BlockSpec((bm, d), lambda i: (i, 0))],
        out_shape=[jax.ShapeDtypeStruct((n, d), F32), jax.ShapeDtypeStruct((n, d), BF16)],
        compiler_params=_params(("parallel",)),
        name="moe_combine",
    )(y_both, y_both, route, x, g.reshape(1, d), b.reshape(1, d))


def _route_plan(route, n_experts, bm):
    n = route.shape[0]
    n_assign = TOP_K_EXPERTS * n
    n_rows = n_assign + n_experts * bm
    flat_e = route[:, :TOP_K_EXPERTS].astype(I32).reshape(n_assign)
    onehot = (flat_e[:, None] == jnp.arange(n_experts, dtype=I32)[None, :]).astype(I32)
    csum = jnp.cumsum(onehot, axis=0)
    rank = jnp.sum(onehot * csum, axis=1) - 1
    counts = csum[-1]
    padded = (counts + bm - 1) // bm * bm
    ends = jnp.cumsum(padded)
    starts = ends - padded
    dest = jnp.sum(onehot * starts[None, :], axis=1) + rank
    src = jnp.zeros((n_rows,), I32).at[dest].set(jnp.arange(n_assign, dtype=I32) // TOP_K_EXPERTS)
    tile_start = jnp.arange(n_rows // bm, dtype=I32) * bm
    tile_expert = jnp.minimum(jnp.sum((tile_start[:, None] >= ends[None, :]).astype(I32), axis=1),
                              n_experts - 1)
    tile_valid = (tile_start < ends[-1]).astype(I32)
    return src, dest, tile_expert, tile_valid


def _rope_tables(positions):
    pos = positions.astype(F32)[..., None]
    inv_h = ROPE_THETA ** (-jnp.arange(0, HEAD_DIM, 2, dtype=F32) / HEAD_DIM)
    inv_i = ROPE_THETA ** (-jnp.arange(0, IDX_DIM, 2, dtype=F32) / IDX_DIM)
    ang_h = pos * inv_h
    ang_i = pos * inv_i
    ch, sh = jnp.cos(ang_h), jnp.sin(ang_h)
    ci, si = jnp.cos(ang_i), jnp.sin(ang_i)
    z32 = jnp.zeros_like(si)
    one64 = jnp.ones(ci.shape[:-1] + (IDX_DIM,), F32)
    z64 = jnp.zeros_like(one64)
    cat = lambda *a: jnp.concatenate(a, axis=-1).reshape(-1, LANES)
    main = (cat(ch, ch), cat(-sh, sh), cat(ci, ci, ci, ci), cat(-si, z32, -si, z32),
            cat(z32, si, z32, si))
    small = (main[0], main[1], cat(ci, ci, one64), cat(-si, z32, z64), cat(z32, si, z64))
    return main, small


def kernel(x, positions, ln_in_g, ln_in_b, w_in, w_proj_a, w_proj_b, w_out, ln_mix_g, ln_mix_b,
           ffn_w_gate, ffn_w_up, ffn_w_down, router_w, moe_w_gate, moe_w_up, moe_w_down,
           ln_ffn_g, ln_ffn_b):
    bsz, s_len, d = x.shape
    depth = w_in.shape[0]
    n = bsz * s_len
    n_heads = d // (2 * HEAD_DIM)
    d_a = n_heads * HEAD_DIM
    d_qi = IDX_HEADS * IDX_DIM
    n_experts = router_w.shape[-1]
    alpha = (2.0 * depth) ** 0.25
    n_sel = min(TOPK_MAX, s_len // 4)
    tq = 256
    assert s_len % tq == 0 and tq % CHUNK == 0 and n_sel <= tq and d_qi % d_a == 0

    off_ki = 3 * d_a + d_qi
    off_qb = off_ki + IDX_DIM + IDX_HEADS
    qb_col_block = (3 * d_a + d_qi) // d_a
    ga_col = 6 * d_a + d_qi

    tabs_main, tabs_small = _rope_tables(positions)
    tri = (lax.broadcasted_iota(I32, (tq, tq), 0) >= lax.broadcasted_iota(I32, (tq, tq), 1)
           ).astype(BF16)

    xf, xb = _input_ln(x.reshape(n, d), ln_in_g, ln_in_b)
    for layer in range(depth):
        w_l = w_in[layer]
        w_main = jnp.concatenate([w_l[:, :off_ki], w_l[:, off_qb:]], axis=1).astype(BF16)
        w_small = jnp.pad(w_l[:, off_ki:off_qb],
                          ((0, 0), (0, LANES - IDX_DIM - IDX_HEADS))).astype(BF16)
        hmain = _inproj(xb, w_main, tabs_main, bn=d_a, rope128_tiles=2,
                        rope64_tiles=(3, 3 + d_qi // d_a), out_dtype=BF16)
        hsmall = _inproj(xb, w_small, tabs_small, bn=LANES, rope128_tiles=0,
                         rope64_tiles=(0, 1), out_dtype=F32)
        h3 = hmain.reshape(bsz, s_len, -1)
        hs3 = hsmall.reshape(bsz, s_len, LANES)

        qat = h3[..., :d_a].reshape(bsz, s_len, n_heads, HEAD_DIM).transpose(0, 2, 3, 1)
        vat = (h3[..., 2 * d_a:3 * d_a].reshape(bsz, s_len // tq, tq, n_heads, HEAD_DIM)
               .transpose(0, 3, 1, 4, 2))
        vat = jnp.concatenate(
            [vat, jnp.ones((bsz, n_heads, s_len // tq, ONES_ROWS, tq), BF16)], axis=3)
        qit = (h3[..., 3 * d_a:3 * d_a + d_qi].reshape(bsz, s_len, IDX_HEADS, IDX_DIM)
               .transpose(0, 2, 3, 1))
        ki = hs3[..., :IDX_DIM].astype(BF16)
        wit = hs3[..., IDX_DIM:IDX_DIM + IDX_HEADS].transpose(0, 2, 1)

        bias = _indexer(qit, ki, wit, tri, tq=tq, n_sel=n_sel)
        oat = _sparse_attn(qat, h3, vat, bias, tq=tq, n_heads=n_heads, k_col_block=1)
        o_a = oat.transpose(0, 3, 1, 2).reshape(n, d_a)
        o_b = _stick_attn(h3, tri, tq=tq, n_heads=n_heads, q_col_block=qb_col_block
                          ).reshape(n, d_a)

        merged = _gated_proj(o_a, o_b, w_proj_a[layer].astype(BF16), w_proj_b[layer].astype(BF16),
                             hmain, ga_col=ga_col)
        xf, xb = _mm_res_ln(merged, w_out[layer].astype(BF16), xf, ln_mix_g[layer],
                            ln_mix_b[layer], alpha=alpha, bm=_pick(n, 512), bk=d)

        i = layer // 2
        if layer % 2 == 0:
            hid = _swiglu_hidden(xb, ffn_w_gate[i].astype(BF16), ffn_w_up[i].astype(BF16))
            f_dim = hid.shape[1]
            bk = max(b for b in (1408, 1024, 512, 256, 128) if f_dim % b == 0)
            xf, xb = _mm_res_ln(hid, ffn_w_down[i].astype(BF16), xf, ln_ffn_g[layer],
                                ln_ffn_b[layer], alpha=alpha, bm=_pick(n, 512), bk=bk)
        else:
            rw = jnp.pad(router_w[i], ((0, 0), (0, LANES - n_experts))).astype(BF16)
            route = _router(xb, rw, n_experts=n_experts)
            bm_e = _pick(n, 512)
            src, dest, tile_expert, tile_valid = _route_plan(route, n_experts, bm_e)
            xs = _row_gather(xf, src)
            hs = _moe_up(xs, moe_w_gate[i].astype(BF16), moe_w_up[i].astype(BF16),
                         tile_expert, tile_valid, bm=bm_e)
            ys = _moe_down(hs, moe_w_down[i].astype(BF16), tile_expert, tile_valid, bm=bm_e)
            y_both = _row_gather(ys, dest.reshape(n, TOP_K_EXPERTS).T.reshape(-1))
            xf, xb = _moe_combine(y_both, route, xf, ln_ffn_g[layer], ln_ffn_b[layer],
                                  alpha=alpha)
    return xf.reshape(bsz, s_len, d)
```

```python
import functools
import math

import jax
import jax.numpy as jnp
from jax import lax
from jax.experimental import pallas as pl
from jax.experimental.pallas import tpu as pltpu

F32 = jnp.float32
BF16 = jnp.bfloat16
I32 = jnp.int32

LN_EPS = 1e-5
HEAD_DIM = 128
IDX_HEADS = 16
IDX_DIM = 64
CHUNK = 64
TOPK_MAX = 256
ROPE_THETA = 10000.0
TOP_K_EXPERTS = 2
LANES = 128
VMEM_LIMIT = 56 * 1024 * 1024
MASK_NEG = -1e30
KEY_NEG_INF = -2139095041
STICK_SKIP_LOG = -100.0
SPARSE_Q_SCALE = math.log2(math.e) / math.sqrt(HEAD_DIM)
ONES_ROWS = 16
INT_MIN = -2147483648


def _params(sem):
    return pltpu.CompilerParams(dimension_semantics=sem, vmem_limit_bytes=VMEM_LIMIT)


def _pick(n, pref):
    b = min(n, pref)
    while n % b:
        b //= 2
    return b


def _layer_norm(v, g, b):
    mu = jnp.mean(v, axis=-1, keepdims=True)
    c = v - mu
    var = jnp.mean(c * c, axis=-1, keepdims=True)
    return c * lax.rsqrt(var + LN_EPS) * g + b


def _ln_kernel(x_ref, g_ref, b_ref, of_ref, ob_ref):
    y = _layer_norm(x_ref[...], g_ref[...], b_ref[...])
    of_ref[...] = y
    ob_ref[...] = y.astype(BF16)


def _input_ln(x2d, g, b):
    n, d = x2d.shape
    bm = _pick(n, 512)
    return pl.pallas_call(
        _ln_kernel,
        grid=(n // bm,),
        in_specs=[pl.BlockSpec((bm, d), lambda i: (i, 0)),
                  pl.BlockSpec((1, d), lambda i: (0, 0)),
                  pl.BlockSpec((1, d), lambda i: (0, 0))],
        out_specs=[pl.BlockSpec((bm, d), lambda i: (i, 0)),
                   pl.BlockSpec((bm, d), lambda i: (i, 0))],
        out_shape=[jax.ShapeDtypeStruct((n, d), F32), jax.ShapeDtypeStruct((n, d), BF16)],
        compiler_params=_params(("parallel",)),
        name="input_ln",
    )(x2d, g.reshape(1, d), b.reshape(1, d))


def _inproj_kernel(x_ref, w_ref, c128_ref, s128_ref, c64_ref, sa64_ref, sb64_ref, o_ref, *,
                   rope128_tiles, rope64_tiles, stick_q_tile):
    j = pl.program_id(1)
    n_heads = o_ref.shape[1] // LANES
    lo64, hi64 = rope64_tiles
    is128 = j < rope128_tiles
    is64 = jnp.logical_and(j >= lo64, j < hi64)

    def project():
        return jnp.dot(x_ref[...], w_ref[...], preferred_element_type=F32)

    @pl.when(is128)
    def _():
        q_scale = jnp.where(j == 0, SPARSE_Q_SCALE, 1.0)
        c = c128_ref[...] * q_scale
        s = s128_ref[...] * q_scale
        y = project()
        for h in range(n_heads):
            yh = y[:, h * LANES:(h + 1) * LANES]
            o_ref[:, h * LANES:(h + 1) * LANES] = (
                yh * c + pltpu.roll(yh, HEAD_DIM // 2, 1) * s).astype(o_ref.dtype)

    @pl.when(is64)
    def _():
        c = c64_ref[...]
        sa = sa64_ref[...]
        sb = sb64_ref[...]
        y = project()
        for h in range(n_heads):
            yh = y[:, h * LANES:(h + 1) * LANES]
            o_ref[:, h * LANES:(h + 1) * LANES] = (
                yh * c + pltpu.roll(yh, LANES - IDX_DIM // 2, 1) * sa
                + pltpu.roll(yh, IDX_DIM // 2, 1) * sb).astype(o_ref.dtype)

    @pl.when(jnp.logical_not(jnp.logical_or(is128, is64)))
    def _():
        o_ref[...] = (project() * jnp.where(j == stick_q_tile, 1.0 / math.sqrt(HEAD_DIM), 1.0)
                      ).astype(o_ref.dtype)


def _inproj(xb, w, tabs, *, bn, rope128_tiles, rope64_tiles, stick_q_tile, out_dtype):
    n, d = xb.shape
    cols = w.shape[1]
    bm = _pick(n, 1024)
    tab_spec = pl.BlockSpec((bm, LANES), lambda i, j: (i, 0))
    return pl.pallas_call(
        functools.partial(_inproj_kernel, rope128_tiles=rope128_tiles, rope64_tiles=rope64_tiles,
                          stick_q_tile=stick_q_tile),
        grid=(n // bm, cols // bn),
        in_specs=[pl.BlockSpec((bm, d), lambda i, j: (i, 0)),
                  pl.BlockSpec((d, bn), lambda i, j: (0, j))] + [tab_spec] * 5,
        out_specs=pl.BlockSpec((bm, bn), lambda i, j: (i, j)),
        out_shape=jax.ShapeDtypeStruct((n, cols), out_dtype),
        compiler_params=_params(("parallel", "arbitrary")),
        name="inproj",
    )(xb, w, *tabs)


def _indexer_kernel(qit_ref, ki_ref, wit_ref, tri_ref, bias_ref, skey_ref, *, tq, n_sel):
    qb = pl.program_id(1)
    n_kt = bias_ref.shape[0] // tq
    q_chunk = (qb * tq + lax.broadcasted_iota(I32, (tq, tq), 1)) // CHUNK
    k_local = lax.broadcasted_iota(I32, (tq, tq), 0)

    def score_tile(kt, carry):
        k0 = pl.multiple_of(kt * tq, tq)
        k_t = ki_ref[pl.ds(k0, tq), :]
        acc = jnp.zeros((tq, tq), F32)
        for h in range(IDX_HEADS):
            lg = jnp.dot(k_t, qit_ref[h], preferred_element_type=F32)
            acc = acc + jnp.maximum(lg, 0.0) * wit_ref[h:h + 1, :]
        acc = jnp.where((k0 + k_local) // CHUNK <= q_chunk, acc, -jnp.inf)
        bits = pltpu.bitcast(acc, I32)
        skey_ref[pl.ds(k0, tq), :] = bits ^ ((bits >> 31) & 0x7FFFFFFF)
        return carry

    lax.fori_loop(0, qb + 1, score_tile, 0)

    def count_ge(cand):
        def body(kt, accs):
            k0 = pl.multiple_of(kt * tq, tq)
            accs = list(accs)
            x = skey_ref[pl.ds(k0, tq), :]
            for r in range(tq // 8):
                m = x[r * 8:(r + 1) * 8, :] >= cand
                a = accs[r % len(accs)]
                accs[r % len(accs)] = jnp.where(m, a + 1, a)
            return tuple(accs)
        accs = lax.fori_loop(0, qb + 1, body, tuple(jnp.zeros((8, tq), I32) for _ in range(4)))
        return (accs[0] + accs[1] + accs[2] + accs[3]).sum(axis=0, keepdims=True)

    def bit_step(i, carry):
        ans, cnt = carry
        cand = ans + lax.shift_left(jnp.int32(1), 31 - i)
        cnt_cand = count_ge(cand)
        keep = cnt_cand >= n_sel
        return jnp.where(keep, cand, ans), jnp.where(keep, cnt_cand, cnt)

    n_bits = jnp.where((qb + 1) * tq > n_sel, 32, 0)
    thr, cnt = lax.fori_loop(0, n_bits, bit_step, (jnp.full((1, tq), INT_MIN, I32),
                                                   jnp.zeros((1, tq), I32)))
    tied = jnp.where(thr > KEY_NEG_INF, jnp.where(cnt > n_sel, 1, 0), 0)
    any_tied = jnp.max(tied) > 0
    thr = jnp.maximum(thr, KEY_NEG_INF + 1)

    @pl.when(jnp.logical_not(any_tied))
    def _():
        def bias_tile(kt, carry):
            k0 = pl.multiple_of(kt * tq, tq)
            sel = skey_ref[pl.ds(k0, tq), :] >= thr
            bias_ref[pl.ds(k0, tq), :] = jnp.where(sel, 0.0, MASK_NEG).astype(BF16)
            return carry

        lax.fori_loop(0, qb + 1, bias_tile, 0)

    @pl.when(any_tied)
    def _():
        need = (n_sel - count_ge(thr + 1)).astype(F32)
        tri = tri_ref[...]

        def bias_tile(kt, seen):
            k0 = pl.multiple_of(kt * tq, tq)
            x = skey_ref[pl.ds(k0, tq), :]
            eq = x == thr
            rank = seen + jnp.dot(tri, jnp.where(eq, 1.0, 0.0).astype(BF16),
                                  preferred_element_type=F32)
            tie_bias = jnp.where(eq, jnp.where(rank <= need, 0.0, MASK_NEG), MASK_NEG)
            bias_ref[pl.ds(k0, tq), :] = jnp.where(x > thr, 0.0, tie_bias).astype(BF16)
            return rank[tq - 1:tq, :]

        lax.fori_loop(0, qb + 1, bias_tile, jnp.zeros((1, tq), F32))

    def fill_tile(kt, carry):
        k0 = pl.multiple_of(kt * tq, tq)
        bias_ref[pl.ds(k0, tq), :] = jnp.full((tq, tq), MASK_NEG, BF16)
        return carry

    lax.fori_loop(qb + 1, n_kt, fill_tile, 0)


def _indexer(qit, ki, wit, tri, *, tq, n_sel):
    bsz, _, _, s_len = qit.shape
    return pl.pallas_call(
        functools.partial(_indexer_kernel, tq=tq, n_sel=n_sel),
        grid=(bsz, s_len // tq),
        in_specs=[pl.BlockSpec((None, IDX_HEADS, IDX_DIM, tq), lambda b, q: (b, 0, 0, q)),
                  pl.BlockSpec((None, s_len, IDX_DIM), lambda b, q: (b, 0, 0)),
                  pl.BlockSpec((None, IDX_HEADS, tq), lambda b, q: (b, 0, q)),
                  pl.BlockSpec((tq, tq), lambda b, q: (0, 0))],
        out_specs=pl.BlockSpec((None, s_len, tq), lambda b, q: (b, 0, q)),
        out_shape=jax.ShapeDtypeStruct((bsz, s_len, s_len), BF16),
        scratch_shapes=[pltpu.VMEM((s_len, tq), I32)],
        compiler_params=_params(("parallel", "arbitrary")),
        name="indexer_topk",
    )(qit, ki, wit, tri)


def _sparse_attn_kernel(qt_ref, k_ref, vt_ref, bias_ref, ot_ref, acc_ref, s_ref, p_ref, *, tq,
                        n_heads):
    qb = pl.program_id(1)
    acc_ref[...] = jnp.zeros(acc_ref.shape, F32)

    def body(kt, ms):
        k0 = pl.multiple_of(kt * tq, tq)
        for h in range(n_heads):
            k_t = k_ref[pl.ds(k0, tq), h * HEAD_DIM:(h + 1) * HEAD_DIM]
            s_ref[h] = jnp.dot(k_t, qt_ref[h], preferred_element_type=F32).astype(BF16)
        bias = bias_ref[pl.ds(k0, tq), :]
        new_ms, alphas = [], []
        for h in range(n_heads):
            s = s_ref[h] + bias
            m_new = jnp.maximum(ms[h], s.max(axis=0, keepdims=True).astype(F32))
            alphas.append(jnp.exp2(ms[h] - m_new))
            p_ref[h] = jnp.exp2(s - m_new.astype(BF16))
            new_ms.append(m_new)
        for h in range(n_heads):
            acc_ref[h] = acc_ref[h] * alphas[h] + jnp.dot(vt_ref[h, kt], p_ref[h],
                                                          preferred_element_type=F32)
        return tuple(new_ms)

    lax.fori_loop(0, qb + 1, body, tuple(jnp.full((1, tq), -jnp.inf, F32)
                                         for _ in range(n_heads)))
    for h in range(n_heads):
        ot_ref[h] = (acc_ref[h, :HEAD_DIM, :] / acc_ref[h, HEAD_DIM:HEAD_DIM + 1, :]
                     ).astype(ot_ref.dtype)


def _sparse_attn(qat, h3, vat, bias, *, tq, n_heads, k_col_block):
    bsz, s_len, _ = h3.shape
    d_a = n_heads * HEAD_DIM
    n_kt = s_len // tq
    v_rows = HEAD_DIM + ONES_ROWS
    assert vat.shape == (bsz, n_heads, n_kt, v_rows, tq)
    return pl.pallas_call(
        functools.partial(_sparse_attn_kernel, tq=tq, n_heads=n_heads),
        grid=(bsz, s_len // tq),
        in_specs=[pl.BlockSpec((None, n_heads, HEAD_DIM, tq), lambda b, q: (b, 0, 0, q)),
                  pl.BlockSpec((None, s_len, d_a), lambda b, q: (b, 0, k_col_block)),
                  pl.BlockSpec((None, n_heads, n_kt, v_rows, tq), lambda b, q: (b, 0, 0, 0, 0)),
                  pl.BlockSpec((None, s_len, tq), lambda b, q: (b, 0, q))],
        out_specs=pl.BlockSpec((None, n_heads, HEAD_DIM, tq), lambda b, q: (b, 0, 0, q)),
        out_shape=jax.ShapeDtypeStruct((bsz, n_heads, HEAD_DIM, s_len), BF16),
        scratch_shapes=[pltpu.VMEM((n_heads, v_rows, tq), F32),
                        pltpu.VMEM((n_heads, tq, tq), BF16),
                        pltpu.VMEM((n_heads, tq, tq), BF16)],
        compiler_params=_params(("parallel", "arbitrary")),
        name="sparse_attn",
    )(qat, h3, vat, bias)


def _stick_kernel(q_ref, k_ref, v_ref, tri_ref, o_ref, kmax_ref, acc_ref, run_ref, z_ref, suf_ref,
                  hi_ref, lo_ref, a_ref, *, tq, n_heads):
    qb = pl.program_id(1)
    s_len, d_b = k_ref.shape
    heads = [slice(h * HEAD_DIM, (h + 1) * HEAD_DIM) for h in range(n_heads)]

    @pl.when(qb == 0)
    def _():
        def body(kt, m8):
            k_t = jnp.abs(k_ref[pl.ds(pl.multiple_of(kt * tq, tq), tq), :].astype(F32))
            return jnp.maximum(m8, k_t.reshape(tq // 8, 8, d_b).max(axis=0))
        m8 = lax.fori_loop(0, s_len // tq, body, jnp.zeros((8, d_b), F32))
        kmax_ref[...] = m8.max(axis=0, keepdims=True)

    z_bounds = [jnp.sum(jnp.abs(q_ref[:, c].astype(F32)) * kmax_ref[:, c], axis=1, keepdims=True)
                for c in heads]
    tri = tri_ref[...]

    def tile_step(k0, strict):
        for h, c in enumerate(heads):
            z_ref[h] = lax.dot_general(q_ref[:, c], k_ref[pl.ds(k0, tq), c],
                                       (((1,), (1,)), ((), ())), preferred_element_type=F32)
        for h in range(n_heads):
            z = z_ref[h]
            log_keep = -(jnp.maximum(z, 0.0) + jnp.log(1.0 + jnp.exp(-jnp.abs(z))))
            if strict is not None:
                log_keep = jnp.where(strict, log_keep, 0.0)
            hi = log_keep.astype(BF16)
            hi_ref[h] = hi
            lo_ref[h] = (log_keep - hi.astype(F32)).astype(BF16)
        for h in range(n_heads):
            suf_ref[h] = (jnp.dot(hi_ref[h], tri, preferred_element_type=F32)
                          + jnp.dot(lo_ref[h], tri, preferred_element_type=F32))
        bound = None
        for h in range(n_heads):
            suffix = suf_ref[h]
            a = jnp.exp(z_ref[h] + suffix + run_ref[h])
            if strict is not None:
                a = jnp.where(strict, a, 0.0)
            a_ref[h] = a.astype(BF16)
            run = run_ref[h] + suffix[:, 0:1]
            run_ref[h] = run
            b_h = jnp.max(run + z_bounds[h])
            bound = b_h if bound is None else jnp.maximum(bound, b_h)
        for h, c in enumerate(heads):
            acc_ref[:, c] += jnp.dot(a_ref[h], v_ref[pl.ds(k0, tq), c],
                                     preferred_element_type=F32)
        return bound

    acc_ref[...] = jnp.zeros(acc_ref.shape, F32)
    run_ref[...] = jnp.zeros(run_ref.shape, F32)
    strict = lax.broadcasted_iota(I32, (tq, tq), 1) < lax.broadcasted_iota(I32, (tq, tq), 0)
    bound0 = tile_step(pl.multiple_of(qb * tq, tq), strict)

    def cond(carry):
        i, bound = carry
        return jnp.logical_and(i <= qb, bound > STICK_SKIP_LOG)

    def body(carry):
        i, _ = carry
        return i + 1, tile_step(pl.multiple_of((qb - i) * tq, tq), None)

    lax.while_loop(cond, body, (jnp.int32(1), bound0))
    o_ref[...] = acc_ref[...].astype(o_ref.dtype)


def _stick_attn(h3, tri, *, tq, n_heads, q_col_block):
    bsz, s_len, _ = h3.shape
    d_b = n_heads * HEAD_DIM
    return pl.pallas_call(
        functools.partial(_stick_kernel, tq=tq, n_heads=n_heads),
        grid=(bsz, s_len // tq),
        in_specs=[pl.BlockSpec((None, tq, d_b), lambda b, q: (b, q, q_col_block)),
                  pl.BlockSpec((None, s_len, d_b), lambda b, q: (b, 0, q_col_block + 1)),
                  pl.BlockSpec((None, s_len, d_b), lambda b, q: (b, 0, q_col_block + 2)),
                  pl.BlockSpec((tq, tq), lambda b, q: (0, 0))],
        out_specs=pl.BlockSpec((None, tq, d_b), lambda b, q: (b, q, 0)),
        out_shape=jax.ShapeDtypeStruct((bsz, s_len, d_b), BF16),
        scratch_shapes=[pltpu.VMEM((1, d_b), F32),
                        pltpu.VMEM((tq, d_b), F32),
                        pltpu.VMEM((n_heads, tq, 1), F32),
                        pltpu.VMEM((n_heads, tq, tq), F32),
                        pltpu.VMEM((n_heads, tq, tq), F32),
                        pltpu.VMEM((n_heads, tq, tq), BF16),
                        pltpu.VMEM((n_heads, tq, tq), BF16),
                        pltpu.VMEM((n_heads, tq, tq), BF16)],
        compiler_params=_params(("arbitrary", "arbitrary")),
        name="stick_attn",
    )(h3, h3, h3, tri)


def _gated_proj_kernel(oa_ref, ob_ref, pa_ref, pb_ref, ga_ref, gb_ref, o_ref):
    ya = jnp.dot(oa_ref[...], pa_ref[...], preferred_element_type=F32)
    yb = jnp.dot(ob_ref[...], pb_ref[...], preferred_element_type=F32)
    sa = 1.0 / (1.0 + jnp.exp(-ga_ref[...].astype(F32)))
    sb = 1.0 / (1.0 + jnp.exp(-gb_ref[...].astype(F32)))
    o_ref[...] = (sa * ya + sb * yb).astype(o_ref.dtype)


def _gated_proj(oa, ob, pa, pb, hmain, *, ga_col):
    n, d_a = oa.shape
    d = pa.shape[1]
    bm = _pick(n, 1024)
    bn = _pick(d, 1024)
    ga_blk = ga_col // bn
    gb_blk = (ga_col + d) // bn
    return pl.pallas_call(
        _gated_proj_kernel,
        grid=(n // bm, d // bn),
        in_specs=[pl.BlockSpec((bm, d_a), lambda i, j: (i, 0)),
                  pl.BlockSpec((bm, d_a), lambda i, j: (i, 0)),
                  pl.BlockSpec((d_a, bn), lambda i, j: (0, j)),
                  pl.BlockSpec((d_a, bn), lambda i, j: (0, j)),
                  pl.BlockSpec((bm, bn), lambda i, j: (i, ga_blk + j)),
                  pl.BlockSpec((bm, bn), lambda i, j: (i, gb_blk + j))],
        out_specs=pl.BlockSpec((bm, bn), lambda i, j: (i, j)),
        out_shape=jax.ShapeDtypeStruct((n, d), BF16),
        compiler_params=_params(("parallel", "arbitrary")),
        name="gated_proj",
    )(oa, ob, pa, pb, hmain, hmain)


def _mm_res_ln_kernel(a_ref, w_ref, x_ref, g_ref, b_ref, of_ref, ob_ref, *, alpha):
    k = pl.program_id(1)

    @pl.when(k == 0)
    def _():
        of_ref[...] = jnp.dot(a_ref[...], w_ref[...], preferred_element_type=F32)

    @pl.when(k > 0)
    def _():
        of_ref[...] += jnp.dot(a_ref[...], w_ref[...], preferred_element_type=F32)

    @pl.when(k == pl.num_programs(1) - 1)
    def _():
        y = _layer_norm(alpha * x_ref[...] + of_ref[...], g_ref[...], b_ref[...])
        of_ref[...] = y
        ob_ref[...] = y.astype(BF16)


def _mm_res_ln(a, w, x, g, b, *, alpha, bm, bk):
    n, kk = a.shape
    d = w.shape[1]
    assert n % bm == 0 and kk % bk == 0
    return pl.pallas_call(
        functools.partial(_mm_res_ln_kernel, alpha=alpha),
        grid=(n // bm, kk // bk),
        in_specs=[pl.BlockSpec((bm, bk), lambda i, k: (i, k)),
                  pl.BlockSpec((bk, d), lambda i, k: (k, 0)),
                  pl.BlockSpec((bm, d), lambda i, k: (i, 0)),
                  pl.BlockSpec((1, d), lambda i, k: (0, 0)),
                  pl.BlockSpec((1, d), lambda i, k: (0, 0))],
        out_specs=[pl.BlockSpec((bm, d), lambda i, k: (i, 0)),
                   pl.BlockSpec((bm, d), lambda i, k: (i, 0))],
        out_shape=[jax.ShapeDtypeStruct((n, d), F32), jax.ShapeDtypeStruct((n, d), BF16)],
        compiler_params=_params(("parallel", "arbitrary")),
        name="mm_res_ln",
    )(a, w, x, g.reshape(1, d), b.reshape(1, d))


def _swiglu(x, wg, wu):
    g = jnp.dot(x, wg, preferred_element_type=F32)
    u = jnp.dot(x, wu, preferred_element_type=F32)
    return g / (1.0 + jnp.exp(-g)) * u


def _swiglu_kernel(x_ref, wg_ref, wu_ref, o_ref):
    o_ref[...] = _swiglu(x_ref[...], wg_ref[...], wu_ref[...]).astype(o_ref.dtype)


def _swiglu_hidden(xb, wg, wu):
    n, d = xb.shape
    f = wg.shape[1]
    bm = _pick(n, 1024)
    bn = _pick(f, 512)
    return pl.pallas_call(
        _swiglu_kernel,
        grid=(n // bm, f // bn),
        in_specs=[pl.BlockSpec((bm, d), lambda i, j: (i, 0)),
                  pl.BlockSpec((d, bn), lambda i, j: (0, j)),
                  pl.BlockSpec((d, bn), lambda i, j: (0, j))],
        out_specs=pl.BlockSpec((bm, bn), lambda i, j: (i, j)),
        out_shape=jax.ShapeDtypeStruct((n, f), BF16),
        compiler_params=_params(("parallel", "arbitrary")),
        name="swiglu_hidden",
    )(xb, wg, wu)


def _router_kernel(x_ref, w_ref, comb_ref, *, n_experts):
    logits = jnp.dot(x_ref[...], w_ref[...], preferred_element_type=F32)
    lane = lax.broadcasted_iota(I32, logits.shape, 1).astype(F32)
    logits = jnp.where(lane < n_experts, logits, -jnp.inf)
    v1 = jnp.max(logits, axis=1, keepdims=True)
    i1 = jnp.min(jnp.where(logits == v1, lane, float(LANES)), axis=1, keepdims=True)
    rest = jnp.where(lane == i1, -jnp.inf, logits)
    v2 = jnp.max(rest, axis=1, keepdims=True)
    i2 = jnp.min(jnp.where(rest == v2, lane, float(LANES)), axis=1, keepdims=True)
    e2 = jnp.exp(v2 - v1)
    g1 = 1.0 / (1.0 + e2)
    g2 = e2 / (1.0 + e2)
    comb_ref[...] = jnp.where(lane == 0.0, i1, jnp.where(lane == 1.0, i2, jnp.where(
        lane == 2.0, g1, jnp.where(lane == 3.0, g2, 0.0))))


def _router(xb, rw, *, n_experts):
    n, d = xb.shape
    bm = _pick(n, 1024)
    return pl.pallas_call(
        functools.partial(_router_kernel, n_experts=n_experts),
        grid=(n // bm,),
        in_specs=[pl.BlockSpec((bm, d), lambda i: (i, 0)),
                  pl.BlockSpec((d, LANES), lambda i: (0, 0))],
        out_specs=pl.BlockSpec((bm, LANES), lambda i: (i, 0)),
        out_shape=jax.ShapeDtypeStruct((n, LANES), F32),
        compiler_params=_params(("parallel",)),
        name="router",
    )(xb, rw)


def _row_gather_kernel(idx_ref, src_ref, dst_ref, sem, *, rows, window):

    def row_copy(r):
        return pltpu.make_async_copy(src_ref.at[pl.ds(idx_ref[r], 1)],
                                     dst_ref.at[pl.ds(r, 1)], sem)

    def prime(r, carry):
        row_copy(r).start()
        return carry

    def steady(r, carry):
        row_copy(r - window).wait()
        row_copy(r).start()
        return carry

    def drain(r, carry):
        row_copy(r).wait()
        return carry

    lax.fori_loop(0, window, prime, 0, unroll=8)
    lax.fori_loop(window, rows, steady, 0, unroll=8)
    lax.fori_loop(rows - window, rows, drain, 0, unroll=8)


def _row_gather(src, idx):
    n_out = idx.shape[0]
    rows = _pick(n_out, 512)
    window = 256
    assert rows >= window and rows % 8 == 0 and window % 8 == 0
    return pl.pallas_call(
        functools.partial(_row_gather_kernel, rows=rows, window=window),
        grid=(n_out // rows,),
        in_specs=[pl.BlockSpec((rows,), lambda i: (i,), memory_space=pltpu.SMEM),
                  pl.BlockSpec(memory_space=pl.ANY)],
        out_specs=pl.BlockSpec((rows,) + src.shape[1:], lambda i: (i, 0)),
        out_shape=jax.ShapeDtypeStruct((n_out,) + src.shape[1:], src.dtype),
        scratch_shapes=[pltpu.SemaphoreType.DMA(())],
        compiler_params=_params(("arbitrary",)),
        name="row_gather",
    )(idx, src)


def _moe_up_kernel(te_ref, tv_ref, x_ref, wg_ref, wu_ref, o_ref):
    t = pl.program_id(1)

    @pl.when(tv_ref[t] != 0)
    def _():
        o_ref[...] = _swiglu(x_ref[...].astype(BF16), wg_ref[...], wu_ref[...]).astype(o_ref.dtype)

    @pl.when(tv_ref[t] == 0)
    def _():
        o_ref[...] = jnp.zeros(o_ref.shape, o_ref.dtype)


def _moe_up(xs, wg, wu, tile_expert, tile_valid, *, bm):
    n_rows, d = xs.shape
    f = wg.shape[2]
    bn = f // 4 if (f // 4) % LANES == 0 else _pick(f, 1024)
    grid_spec = pltpu.PrefetchScalarGridSpec(
        num_scalar_prefetch=2,
        grid=(f // bn, n_rows // bm),
        in_specs=[pl.BlockSpec((bm, d), lambda j, t, te, tv: (t, 0)),
                  pl.BlockSpec((None, d, bn), lambda j, t, te, tv: (te[t], 0, j)),
                  pl.BlockSpec((None, d, bn), lambda j, t, te, tv: (te[t], 0, j))],
        out_specs=pl.BlockSpec((bm, bn), lambda j, t, te, tv: (t, j)))
    return pl.pallas_call(
        _moe_up_kernel,
        grid_spec=grid_spec,
        out_shape=jax.ShapeDtypeStruct((n_rows, f), BF16),
        compiler_params=_params(("arbitrary", "arbitrary")),
        name="moe_up",
    )(tile_expert, tile_valid, xs, wg, wu)


def _moe_down_kernel(te_ref, tv_ref, h_ref, w_ref, o_ref):
    t = pl.program_id(0)

    @pl.when(pl.program_id(1) == 0)
    def _():
        o_ref[...] = jnp.zeros(o_ref.shape, o_ref.dtype)

    @pl.when(tv_ref[t] != 0)
    def _():
        o_ref[...] += jnp.dot(h_ref[...], w_ref[...], preferred_element_type=F32)


def _moe_down(hs, wd, tile_expert, tile_valid, *, bm):
    n_rows, f = hs.shape
    d = wd.shape[2]
    bk = f // 2 if (f // 2) % LANES == 0 else f
    grid_spec = pltpu.PrefetchScalarGridSpec(
        num_scalar_prefetch=2,
        grid=(n_rows // bm, f // bk),
        in_specs=[pl.BlockSpec((bm, bk), lambda t, k, te, tv: (t, k)),
                  pl.BlockSpec((None, bk, d), lambda t, k, te, tv: (te[t], k, 0))],
        out_specs=pl.BlockSpec((bm, d), lambda t, k, te, tv: (t, 0)))
    return pl.pallas_call(
        _moe_down_kernel,
        grid_spec=grid_spec,
        out_shape=jax.ShapeDtypeStruct((n_rows, d), F32),
        compiler_params=_params(("arbitrary", "arbitrary")),
        name="moe_down",
    )(tile_expert, tile_valid, hs, wd)


def _moe_combine_kernel(y1_ref, y2_ref, r_ref, x_ref, g_ref, b_ref, of_ref, ob_ref, *, alpha):
    route = r_ref[...]
    y = route[:, 2:3] * y1_ref[...] + route[:, 3:4] * y2_ref[...]
    out = _layer_norm(alpha * x_ref[...] + y, g_ref[...], b_ref[...])
    of_ref[...] = out
    ob_ref[...] = out.astype(BF16)


def _moe_combine(y_both, route, x, g, b, *, alpha):
    n, d = x.shape
    bm = _pick(n, 256)
    nb = n // bm
    return pl.pallas_call(
        functools.partial(_moe_combine_kernel, alpha=alpha),
        grid=(nb,),
        in_specs=[pl.BlockSpec((bm, d), lambda i: (i, 0)),
                  pl.BlockSpec((bm, d), lambda i: (i + nb, 0)),
                  pl.BlockSpec((bm, LANES), lambda i: (i, 0)),
                  pl.BlockSpec((bm, d), lambda i: (i, 0)),
                  pl.BlockSpec((1, d), lambda i: (0, 0)),
                  pl.BlockSpec((1, d), lambda i: (0, 0))],
        out_specs=[pl.BlockSpec((bm, d), lambda i: (i, 0)),
                   pl.BlockSpec((bm, d), lambda i: (i, 0))],
        out_shape=[jax.ShapeDtypeStruct((n, d), F32), jax.ShapeDtypeStruct((n, d), BF16)],
        compiler_params=_params(("parallel",)),
        name="moe_combine",
    )(y_both, y_both, route, x, g.reshape(1, d), b.reshape(1, d))


def _route_plan(route, n_experts, bm):
    n = route.shape[0]
    n_assign = TOP_K_EXPERTS * n
    n_rows = n_assign + n_experts * bm
    flat_e = route[:, :TOP_K_EXPERTS].astype(I32).reshape(n_assign)
    onehot = (flat_e[:, None] == jnp.arange(n_experts, dtype=I32)[None, :]).astype(I32)
    csum = jnp.cumsum(onehot, axis=0)
    rank = jnp.sum(onehot * csum, axis=1) - 1
    counts = csum[-1]
    padded = (counts + bm - 1) // bm * bm
    ends = jnp.cumsum(padded)
    starts = ends - padded
    dest = jnp.sum(onehot * starts[None, :], axis=1) + rank
    src = jnp.zeros((n_rows,), I32).at[dest].set(jnp.arange(n_assign, dtype=I32) // TOP_K_EXPERTS)
    tile_start = jnp.arange(n_rows // bm, dtype=I32) * bm
    tile_expert = jnp.minimum(jnp.sum((tile_start[:, None] >= ends[None, :]).astype(I32), axis=1),
                              n_experts - 1)
    tile_valid = (tile_start < ends[-1]).astype(I32)
    return src, dest, tile_expert, tile_valid


def _rope_tables(positions):
    pos = positions.astype(F32)[..., None]
    inv_h = ROPE_THETA ** (-jnp.arange(0, HEAD_DIM, 2, dtype=F32) / HEAD_DIM)
    inv_i = ROPE_THETA ** (-jnp.arange(0, IDX_DIM, 2, dtype=F32) / IDX_DIM)
    ang_h = pos * inv_h
    ang_i = pos * inv_i
    ch, sh = jnp.cos(ang_h), jnp.sin(ang_h)
    ci, si = jnp.cos(ang_i), jnp.sin(ang_i)
    z32 = jnp.zeros_like(si)
    one64 = jnp.ones(ci.shape[:-1] + (IDX_DIM,), F32)
    z64 = jnp.zeros_like(one64)
    cat = lambda *a: jnp.concatenate(a, axis=-1).reshape(-1, LANES)
    main = (cat(ch, ch), cat(-sh, sh), cat(ci, ci, ci, ci), cat(-si, z32, -si, z32),
            cat(z32, si, z32, si))
    small = (main[0], main[1], cat(ci, ci, one64), cat(-si, z32, z64), cat(z32, si, z64))
    return main, small


def kernel(x, positions, ln_in_g, ln_in_b, w_in, w_proj_a, w_proj_b, w_out, ln_mix_g, ln_mix_b,
           ffn_w_gate, ffn_w_up, ffn_w_down, router_w, moe_w_gate, moe_w_up, moe_w_down,
           ln_ffn_g, ln_ffn_b):
    bsz, s_len, d = x.shape
    depth = w_in.shape[0]
    n = bsz * s_len
    n_heads = d // (2 * HEAD_DIM)
    d_a = n_heads * HEAD_DIM
    d_qi = IDX_HEADS * IDX_DIM
    n_experts = router_w.shape[-1]
    alpha = (2.0 * depth) ** 0.25
    n_sel = min(TOPK_MAX, s_len // 4)
    tq = 256
    assert s_len % tq == 0 and tq % CHUNK == 0 and n_sel <= tq and d_qi % d_a == 0

    off_ki = 3 * d_a + d_qi
    off_qb = off_ki + IDX_DIM + IDX_HEADS
    qb_col_block = (3 * d_a + d_qi) // d_a
    ga_col = 6 * d_a + d_qi

    tabs_main, tabs_small = _rope_tables(positions)
    tri = (lax.broadcasted_iota(I32, (tq, tq), 0) >= lax.broadcasted_iota(I32, (tq, tq), 1)
           ).astype(BF16)

    xf, xb = _input_ln(x.reshape(n, d), ln_in_g, ln_in_b)
    for layer in range(depth):
        w_l = w_in[layer]
        w_main = jnp.concatenate([w_l[:, :off_ki], w_l[:, off_qb:]], axis=1).astype(BF16)
        w_small = jnp.pad(w_l[:, off_ki:off_qb],
                          ((0, 0), (0, LANES - IDX_DIM - IDX_HEADS))).astype(BF16)
        hmain = _inproj(xb, w_main, tabs_main, bn=d_a, rope128_tiles=2,
                        rope64_tiles=(3, 3 + d_qi // d_a), stick_q_tile=qb_col_block,
                        out_dtype=BF16)
        hsmall = _inproj(xb, w_small, tabs_small, bn=LANES, rope128_tiles=0,
                         rope64_tiles=(0, 1), stick_q_tile=-1, out_dtype=F32)
        h3 = hmain.reshape(bsz, s_len, -1)
        hs3 = hsmall.reshape(bsz, s_len, LANES)

        qat = h3[..., :d_a].reshape(bsz, s_len, n_heads, HEAD_DIM).transpose(0, 2, 3, 1)
        vat = (h3[..., 2 * d_a:3 * d_a].reshape(bsz, s_len // tq, tq, n_heads, HEAD_DIM)
               .transpose(0, 3, 1, 4, 2))
        vat = jnp.concatenate(
            [vat, jnp.ones((bsz, n_heads, s_len // tq, ONES_ROWS, tq), BF16)], axis=3)
        qit = (h3[..., 3 * d_a:3 * d_a + d_qi].reshape(bsz, s_len, IDX_HEADS, IDX_DIM)
               .transpose(0, 2, 3, 1))
        ki = hs3[..., :IDX_DIM].astype(BF16)
        wit = hs3[..., IDX_DIM:IDX_DIM + IDX_HEADS].transpose(0, 2, 1)

        bias = _indexer(qit, ki, wit, tri, tq=tq, n_sel=n_sel)
        oat = _sparse_attn(qat, h3, vat, bias, tq=tq, n_heads=n_heads, k_col_block=1)
        o_a = oat.transpose(0, 3, 1, 2).reshape(n, d_a)
        o_b = _stick_attn(h3, tri, tq=tq, n_heads=n_heads, q_col_block=qb_col_block
                          ).reshape(n, d_a)

        merged = _gated_proj(o_a, o_b, w_proj_a[layer].astype(BF16), w_proj_b[layer].astype(BF16),
                             hmain, ga_col=ga_col)
        xf, xb = _mm_res_ln(merged, w_out[layer].astype(BF16), xf, ln_mix_g[layer],
                            ln_mix_b[layer], alpha=alpha, bm=_pick(n, 512), bk=d)

        i = layer // 2
        if layer % 2 == 0:
            hid = _swiglu_hidden(xb, ffn_w_gate[i].astype(BF16), ffn_w_up[i].astype(BF16))
            f_dim = hid.shape[1]
            bk = f_dim // 2 if (f_dim // 2) % LANES == 0 else f_dim
            xf, xb = _mm_res_ln(hid, ffn_w_down[i].astype(BF16), xf, ln_ffn_g[layer],
                                ln_ffn_b[layer], alpha=alpha, bm=_pick(n, 512), bk=bk)
        else:
            rw = jnp.pad(router_w[i], ((0, 0), (0, LANES - n_experts))).astype(BF16)
            route = _router(xb, rw, n_experts=n_experts)
            bm_e = _pick(n, 512)
            src, dest, tile_expert, tile_valid = _route_plan(route, n_experts, bm_e)
            xs = _row_gather(xf, src)
            hs = _moe_up(xs, moe_w_gate[i].astype(BF16), moe_w_up[i].astype(BF16),
                         tile_expert, tile_valid, bm=bm_e)
            ys = _moe_down(hs, moe_w_down[i].astype(BF16), tile_expert, tile_valid, bm=bm_e)
            y_both = _row_gather(ys, dest.reshape(n, TOP_K_EXPERTS).T.reshape(-1))
            xf, xb = _moe_combine(y_both, route, xf, ln_ffn_g[layer], ln_ffn_b[layer],
                                  alpha=alpha)
    return xf.reshape(bsz, s_len, d)
```

```python
import functools
import math

import jax
import jax.numpy as jnp
from jax import lax
from jax.experimental import pallas as pl
from jax.experimental.pallas import tpu as pltpu

F32 = jnp.float32
BF16 = jnp.bfloat16
I32 = jnp.int32

LN_EPS = 1e-5
HEAD_DIM = 128
IDX_HEADS = 16
IDX_DIM = 64
CHUNK = 64
TOPK_MAX = 256
ROPE_THETA = 10000.0
TOP_K_EXPERTS = 2
LANES = 128
VMEM_LIMIT = 56 * 1024 * 1024
MASK_NEG = -1e30
KEY_NEG_INF = -2139095041
STICK_SKIP_LOG = -100.0
SPARSE_Q_SCALE = math.log2(math.e) / math.sqrt(HEAD_DIM)
ONES_ROWS = 16
INT_MIN = -2147483648


def _params(sem):
    return pltpu.CompilerParams(dimension_semantics=sem, vmem_limit_bytes=VMEM_LIMIT)


def _pick(n, pref):
    b = min(n, pref)
    while n % b:
        b //= 2
    return b


def _layer_norm(v, g, b):
    mu = jnp.mean(v, axis=-1, keepdims=True)
    c = v - mu
    var = jnp.mean(c * c, axis=-1, keepdims=True)
    return c * lax.rsqrt(var + LN_EPS) * g + b


def _ln_kernel(x_ref, g_ref, b_ref, of_ref, ob_ref):
    y = _layer_norm(x_ref[...], g_ref[...], b_ref[...])
    of_ref[...] = y
    ob_ref[...] = y.astype(BF16)


def _input_ln(x2d, g, b):
    n, d = x2d.shape
    bm = _pick(n, 512)
    return pl.pallas_call(
        _ln_kernel,
        grid=(n // bm,),
        in_specs=[pl.BlockSpec((bm, d), lambda i: (i, 0)),
                  pl.BlockSpec((1, d), lambda i: (0, 0)),
                  pl.BlockSpec((1, d), lambda i: (0, 0))],
        out_specs=[pl.BlockSpec((bm, d), lambda i: (i, 0)),
                   pl.BlockSpec((bm, d), lambda i: (i, 0))],
        out_shape=[jax.ShapeDtypeStruct((n, d), F32), jax.ShapeDtypeStruct((n, d), BF16)],
        compiler_params=_params(("parallel",)),
        name="input_ln",
    )(x2d, g.reshape(1, d), b.reshape(1, d))


def _inproj_kernel(x_ref, w_ref, c128_ref, s128_ref, c64_ref, sa64_ref, sb64_ref, o_ref, *,
                   rope128_tiles, rope64_tiles, stick_q_tile):
    j = pl.program_id(1)
    n_heads = o_ref.shape[1] // LANES
    lo64, hi64 = rope64_tiles
    is128 = j < rope128_tiles
    is64 = jnp.logical_and(j >= lo64, j < hi64)

    def project():
        return jnp.dot(x_ref[...], w_ref[...], preferred_element_type=F32)

    @pl.when(is128)
    def _():
        q_scale = jnp.where(j == 0, SPARSE_Q_SCALE, 1.0)
        c = c128_ref[...] * q_scale
        s = s128_ref[...] * q_scale
        y = project()
        for h in range(n_heads):
            yh = y[:, h * LANES:(h + 1) * LANES]
            o_ref[:, h * LANES:(h + 1) * LANES] = (
                yh * c + pltpu.roll(yh, HEAD_DIM // 2, 1) * s).astype(o_ref.dtype)

    @pl.when(is64)
    def _():
        c = c64_ref[...]
        sa = sa64_ref[...]
        sb = sb64_ref[...]
        y = project()
        for h in range(n_heads):
            yh = y[:, h * LANES:(h + 1) * LANES]
            o_ref[:, h * LANES:(h + 1) * LANES] = (
                yh * c + pltpu.roll(yh, LANES - IDX_DIM // 2, 1) * sa
                + pltpu.roll(yh, IDX_DIM // 2, 1) * sb).astype(o_ref.dtype)

    @pl.when(jnp.logical_not(jnp.logical_or(is128, is64)))
    def _():
        o_ref[...] = (project() * jnp.where(j == stick_q_tile, 1.0 / math.sqrt(HEAD_DIM), 1.0)
                      ).astype(o_ref.dtype)


def _inproj(xb, w, tabs, *, bn, rope128_tiles, rope64_tiles, stick_q_tile, out_dtype):
    n, d = xb.shape
    cols = w.shape[1]
    bm = _pick(n, 1024)
    tab_spec = pl.BlockSpec((bm, LANES), lambda i, j: (i, 0))
    return pl.pallas_call(
        functools.partial(_inproj_kernel, rope128_tiles=rope128_tiles, rope64_tiles=rope64_tiles,
                          stick_q_tile=stick_q_tile),
        grid=(n // bm, cols // bn),
        in_specs=[pl.BlockSpec((bm, d), lambda i, j: (i, 0)),
                  pl.BlockSpec((d, bn), lambda i, j: (0, j))] + [tab_spec] * 5,
        out_specs=pl.BlockSpec((bm, bn), lambda i, j: (i, j)),
        out_shape=jax.ShapeDtypeStruct((n, cols), out_dtype),
        compiler_params=_params(("parallel", "arbitrary")),
        name="inproj",
    )(xb, w, *tabs)


def _indexer_kernel(qit_ref, ki_ref, wit_ref, tri_ref, bias_ref, skey_ref, *, tq, n_sel):
    qb = pl.program_id(1)
    n_kt = bias_ref.shape[0] // tq
    q_chunk = (qb * tq + lax.broadcasted_iota(I32, (tq, tq), 1)) // CHUNK
    k_local = lax.broadcasted_iota(I32, (tq, tq), 0)

    def score_tile(kt):
        k0 = pl.multiple_of(kt * tq, tq)
        k_t = ki_ref[pl.ds(k0, tq), :]
        acc = jnp.zeros((tq, tq), F32)
        for h in range(IDX_HEADS):
            lg = jnp.dot(k_t, qit_ref[h], preferred_element_type=F32)
            acc = acc + jnp.maximum(lg, 0.0) * wit_ref[h:h + 1, :]
        acc = jnp.where((k0 + k_local) // CHUNK <= q_chunk, acc, -jnp.inf)
        bits = pltpu.bitcast(acc, I32)
        skey_ref[pl.ds(k0, tq), :] = bits ^ ((bits >> 31) & 0x7FFFFFFF)

    odd = (qb + 1) % 2

    @pl.when(odd == 1)
    def _():
        score_tile(0)

    def score_pair(i, carry):
        score_tile(odd + 2 * i)
        score_tile(odd + 2 * i + 1)
        return carry

    lax.fori_loop(0, (qb + 1) // 2, score_pair, 0)

    def count_ge(cand):
        def body(kt, accs):
            k0 = pl.multiple_of(kt * tq, tq)
            accs = list(accs)
            x = skey_ref[pl.ds(k0, tq), :]
            for r in range(tq // 8):
                m = x[r * 8:(r + 1) * 8, :] >= cand
                a = accs[r % len(accs)]
                accs[r % len(accs)] = jnp.where(m, a + 1, a)
            return tuple(accs)
        accs = lax.fori_loop(0, qb + 1, body, tuple(jnp.zeros((8, tq), I32) for _ in range(4)))
        return (accs[0] + accs[1] + accs[2] + accs[3]).sum(axis=0, keepdims=True)

    def bit_step(i, carry):
        ans, cnt = carry
        cand = ans + lax.shift_left(jnp.int32(1), 31 - i)
        cnt_cand = count_ge(cand)
        keep = cnt_cand >= n_sel
        return jnp.where(keep, cand, ans), jnp.where(keep, cnt_cand, cnt)

    n_bits = jnp.where((qb + 1) * tq > n_sel, 32, 0)
    thr, cnt = lax.fori_loop(0, n_bits, bit_step, (jnp.full((1, tq), INT_MIN, I32),
                                                   jnp.zeros((1, tq), I32)))
    tied = jnp.where(thr > KEY_NEG_INF, jnp.where(cnt > n_sel, 1, 0), 0)
    any_tied = jnp.max(tied) > 0
    thr = jnp.maximum(thr, KEY_NEG_INF + 1)

    @pl.when(jnp.logical_not(any_tied))
    def _():
        def bias_tile(kt, carry):
            k0 = pl.multiple_of(kt * tq, tq)
            sel = skey_ref[pl.ds(k0, tq), :] >= thr
            bias_ref[pl.ds(k0, tq), :] = jnp.where(sel, 0.0, MASK_NEG).astype(BF16)
            return carry

        lax.fori_loop(0, qb + 1, bias_tile, 0)

    @pl.when(any_tied)
    def _():
        need = (n_sel - count_ge(thr + 1)).astype(F32)
        tri = tri_ref[...]

        def bias_tile(kt, seen):
            k0 = pl.multiple_of(kt * tq, tq)
            x = skey_ref[pl.ds(k0, tq), :]
            eq = x == thr
            rank = seen + jnp.dot(tri, jnp.where(eq, 1.0, 0.0).astype(BF16),
                                  preferred_element_type=F32)
            tie_bias = jnp.where(eq, jnp.where(rank <= need, 0.0, MASK_NEG), MASK_NEG)
            bias_ref[pl.ds(k0, tq), :] = jnp.where(x > thr, 0.0, tie_bias).astype(BF16)
            return rank[tq - 1:tq, :]

        lax.fori_loop(0, qb + 1, bias_tile, jnp.zeros((1, tq), F32))

    def fill_tile(kt, carry):
        k0 = pl.multiple_of(kt * tq, tq)
        bias_ref[pl.ds(k0, tq), :] = jnp.full((tq, tq), MASK_NEG, BF16)
        return carry

    lax.fori_loop(qb + 1, n_kt, fill_tile, 0)


def _indexer(qit, ki, wit, tri, *, tq, n_sel):
    bsz, _, _, s_len = qit.shape
    return pl.pallas_call(
        functools.partial(_indexer_kernel, tq=tq, n_sel=n_sel),
        grid=(bsz, s_len // tq),
        in_specs=[pl.BlockSpec((None, IDX_HEADS, IDX_DIM, tq), lambda b, q: (b, 0, 0, q)),
                  pl.BlockSpec((None, s_len, IDX_DIM), lambda b, q: (b, 0, 0)),
                  pl.BlockSpec((None, IDX_HEADS, tq), lambda b, q: (b, 0, q)),
                  pl.BlockSpec((tq, tq), lambda b, q: (0, 0))],
        out_specs=pl.BlockSpec((None, s_len, tq), lambda b, q: (b, 0, q)),
        out_shape=jax.ShapeDtypeStruct((bsz, s_len, s_len), BF16),
        scratch_shapes=[pltpu.VMEM((s_len, tq), I32)],
        compiler_params=_params(("parallel", "arbitrary")),
        name="indexer_topk",
    )(qit, ki, wit, tri)


def _sparse_attn_kernel(qt_ref, k_ref, vt_ref, bias_ref, ot_ref, acc_ref, s_ref, p_ref, *, tq,
                        n_heads):
    qb = pl.program_id(1)
    acc_ref[...] = jnp.zeros(acc_ref.shape, F32)

    def tile(kt, ms, slot):
        k0 = pl.multiple_of(kt * tq, tq)
        for h in range(n_heads):
            k_t = k_ref[pl.ds(k0, tq), h * HEAD_DIM:(h + 1) * HEAD_DIM]
            s_ref[slot, h] = jnp.dot(k_t, qt_ref[h], preferred_element_type=F32).astype(BF16)
        bias = bias_ref[pl.ds(k0, tq), :]
        new_ms, alphas = [], []
        for h in range(n_heads):
            s = s_ref[slot, h] + bias
            m_new = jnp.maximum(ms[h], s.max(axis=0, keepdims=True).astype(F32))
            alphas.append(jnp.exp2(ms[h] - m_new))
            p_ref[slot, h] = jnp.exp2(s - m_new.astype(BF16))
            new_ms.append(m_new)
        for h in range(n_heads):
            acc_ref[h] = acc_ref[h] * alphas[h] + jnp.dot(vt_ref[h, kt], p_ref[slot, h],
                                                          preferred_element_type=F32)
        return tuple(new_ms)

    odd = (qb + 1) % 2
    ms = tuple(jnp.full((1, tq), -jnp.inf, F32) for _ in range(n_heads))
    ms = lax.cond(odd == 1, lambda m: tile(0, m, 0), lambda m: m, ms)

    def pair(i, ms):
        kt = odd + 2 * i
        return tile(kt + 1, tile(kt, ms, 0), 1)

    lax.fori_loop(0, (qb + 1) // 2, pair, ms)
    for h in range(n_heads):
        ot_ref[h] = (acc_ref[h, :HEAD_DIM, :] / acc_ref[h, HEAD_DIM:HEAD_DIM + 1, :]
                     ).astype(ot_ref.dtype)


def _sparse_attn(qat, h3, vat, bias, *, tq, n_heads, k_col_block):
    bsz, s_len, _ = h3.shape
    d_a = n_heads * HEAD_DIM
    n_kt = s_len // tq
    v_rows = HEAD_DIM + ONES_ROWS
    assert vat.shape == (bsz, n_heads, n_kt, v_rows, tq)
    return pl.pallas_call(
        functools.partial(_sparse_attn_kernel, tq=tq, n_heads=n_heads),
        grid=(bsz, s_len // tq),
        in_specs=[pl.BlockSpec((None, n_heads, HEAD_DIM, tq), lambda b, q: (b, 0, 0, q)),
                  pl.BlockSpec((None, s_len, d_a), lambda b, q: (b, 0, k_col_block)),
                  pl.BlockSpec((None, n_heads, n_kt, v_rows, tq), lambda b, q: (b, 0, 0, 0, 0)),
                  pl.BlockSpec((None, s_len, tq), lambda b, q: (b, 0, q))],
        out_specs=pl.BlockSpec((None, n_heads, HEAD_DIM, tq), lambda b, q: (b, 0, 0, q)),
        out_shape=jax.ShapeDtypeStruct((bsz, n_heads, HEAD_DIM, s_len), BF16),
        scratch_shapes=[pltpu.VMEM((n_heads, v_rows, tq), F32),
                        pltpu.VMEM((2, n_heads, tq, tq), BF16),
                        pltpu.VMEM((2, n_heads, tq, tq), BF16)],
        compiler_params=_params(("parallel", "arbitrary")),
        name="sparse_attn",
    )(qat, h3, vat, bias)


def _stick_kernel(q_ref, k_ref, v_ref, tri_ref, o_ref, kmax_ref, acc_ref, run_ref, z_ref, suf_ref,
                  hi_ref, lo_ref, a_ref, *, tq, n_heads):
    qb = pl.program_id(1)
    s_len, d_b = k_ref.shape
    heads = [slice(h * HEAD_DIM, (h + 1) * HEAD_DIM) for h in range(n_heads)]

    @pl.when(qb == 0)
    def _():
        def body(kt, m8):
            k_t = jnp.abs(k_ref[pl.ds(pl.multiple_of(kt * tq, tq), tq), :].astype(F32))
            return jnp.maximum(m8, k_t.reshape(tq // 8, 8, d_b).max(axis=0))
        m8 = lax.fori_loop(0, s_len // tq, body, jnp.zeros((8, d_b), F32))
        kmax_ref[...] = m8.max(axis=0, keepdims=True)

    z_bounds = [jnp.sum(jnp.abs(q_ref[:, c].astype(F32)) * kmax_ref[:, c], axis=1, keepdims=True)
                for c in heads]
    tri = tri_ref[...]

    def tile_step(k0, strict):
        for h, c in enumerate(heads):
            z_ref[h] = lax.dot_general(q_ref[:, c], k_ref[pl.ds(k0, tq), c],
                                       (((1,), (1,)), ((), ())), preferred_element_type=F32)
        for h in range(n_heads):
            z = z_ref[h]
            log_keep = -(jnp.maximum(z, 0.0) + jnp.log(1.0 + jnp.exp(-jnp.abs(z))))
            if strict is not None:
                log_keep = jnp.where(strict, log_keep, 0.0)
            hi = log_keep.astype(BF16)
            hi_ref[h] = hi
            lo_ref[h] = (log_keep - hi.astype(F32)).astype(BF16)
        for h in range(n_heads):
            suf_ref[h] = (jnp.dot(hi_ref[h], tri, preferred_element_type=F32)
                          + jnp.dot(lo_ref[h], tri, preferred_element_type=F32))
        bound = None
        for h in range(n_heads):
            suffix = suf_ref[h]
            a = jnp.exp(z_ref[h] + suffix + run_ref[h])
            if strict is not None:
                a = jnp.where(strict, a, 0.0)
            a_ref[h] = a.astype(BF16)
            run = run_ref[h] + suffix[:, 0:1]
            run_ref[h] = run
            b_h = jnp.max(run + z_bounds[h])
            bound = b_h if bound is None else jnp.maximum(bound, b_h)
        for h, c in enumerate(heads):
            acc_ref[:, c] += jnp.dot(a_ref[h], v_ref[pl.ds(k0, tq), c],
                                     preferred_element_type=F32)
        return bound

    acc_ref[...] = jnp.zeros(acc_ref.shape, F32)
    run_ref[...] = jnp.zeros(run_ref.shape, F32)
    strict = lax.broadcasted_iota(I32, (tq, tq), 1) < lax.broadcasted_iota(I32, (tq, tq), 0)
    bound0 = tile_step(pl.multiple_of(qb * tq, tq), strict)

    def cond(carry):
        i, bound = carry
        return jnp.logical_and(i <= qb, bound > STICK_SKIP_LOG)

    def body(carry):
        i, _ = carry
        return i + 1, tile_step(pl.multiple_of((qb - i) * tq, tq), None)

    lax.while_loop(cond, body, (jnp.int32(1), bound0))
    o_ref[...] = acc_ref[...].astype(o_ref.dtype)


def _stick_attn(h3, tri, *, tq, n_heads, q_col_block):
    bsz, s_len, _ = h3.shape
    d_b = n_heads * HEAD_DIM
    return pl.pallas_call(
        functools.partial(_stick_kernel, tq=tq, n_heads=n_heads),
        grid=(bsz, s_len // tq),
        in_specs=[pl.BlockSpec((None, tq, d_b), lambda b, q: (b, q, q_col_block)),
                  pl.BlockSpec((None, s_len, d_b), lambda b, q: (b, 0, q_col_block + 1)),
                  pl.BlockSpec((None, s_len, d_b), lambda b, q: (b, 0, q_col_block + 2)),
                  pl.BlockSpec((tq, tq), lambda b, q: (0, 0))],
        out_specs=pl.BlockSpec((None, tq, d_b), lambda b, q: (b, q, 0)),
        out_shape=jax.ShapeDtypeStruct((bsz, s_len, d_b), BF16),
        scratch_shapes=[pltpu.VMEM((1, d_b), F32),
                        pltpu.VMEM((tq, d_b), F32),
                        pltpu.VMEM((n_heads, tq, 1), F32),
                        pltpu.VMEM((n_heads, tq, tq), F32),
                        pltpu.VMEM((n_heads, tq, tq), F32),
                        pltpu.VMEM((n_heads, tq, tq), BF16),
                        pltpu.VMEM((n_heads, tq, tq), BF16),
                        pltpu.VMEM((n_heads, tq, tq), BF16)],
        compiler_params=_params(("arbitrary", "arbitrary")),
        name="stick_attn",
    )(h3, h3, h3, tri)


def _gated_proj_kernel(oa_ref, ob_ref, pa_ref, pb_ref, ga_ref, gb_ref, o_ref):
    ya = jnp.dot(oa_ref[...], pa_ref[...], preferred_element_type=F32)
    yb = jnp.dot(ob_ref[...], pb_ref[...], preferred_element_type=F32)
    sa = 1.0 / (1.0 + jnp.exp(-ga_ref[...].astype(F32)))
    sb = 1.0 / (1.0 + jnp.exp(-gb_ref[...].astype(F32)))
    o_ref[...] = (sa * ya + sb * yb).astype(o_ref.dtype)


def _gated_proj(oa, ob, pa, pb, hmain, *, ga_col):
    n, d_a = oa.shape
    d = pa.shape[1]
    bm = _pick(n, 1024)
    bn = _pick(d, 1024)
    ga_blk = ga_col // bn
    gb_blk = (ga_col + d) // bn
    return pl.pallas_call(
        _gated_proj_kernel,
        grid=(n // bm, d // bn),
        in_specs=[pl.BlockSpec((bm, d_a), lambda i, j: (i, 0)),
                  pl.BlockSpec((bm, d_a), lambda i, j: (i, 0)),
                  pl.BlockSpec((d_a, bn), lambda i, j: (0, j)),
                  pl.BlockSpec((d_a, bn), lambda i, j: (0, j)),
                  pl.BlockSpec((bm, bn), lambda i, j: (i, ga_blk + j)),
                  pl.BlockSpec((bm, bn), lambda i, j: (i, gb_blk + j))],
        out_specs=pl.BlockSpec((bm, bn), lambda i, j: (i, j)),
        out_shape=jax.ShapeDtypeStruct((n, d), BF16),
        compiler_params=_params(("parallel", "arbitrary")),
        name="gated_proj",
    )(oa, ob, pa, pb, hmain, hmain)


def _mm_res_ln_kernel(a_ref, w_ref, x_ref, g_ref, b_ref, of_ref, ob_ref, *, alpha):
    k = pl.program_id(1)

    @pl.when(k == 0)
    def _():
        of_ref[...] = jnp.dot(a_ref[...], w_ref[...], preferred_element_type=F32)

    @pl.when(k > 0)
    def _():
        of_ref[...] += jnp.dot(a_ref[...], w_ref[...], preferred_element_type=F32)

    @pl.when(k == pl.num_programs(1) - 1)
    def _():
        y = _layer_norm(alpha * x_ref[...] + of_ref[...], g_ref[...], b_ref[...])
        of_ref[...] = y
        ob_ref[...] = y.astype(BF16)


def _mm_res_ln(a, w, x, g, b, *, alpha, bm, bk):
    n, kk = a.shape
    d = w.shape[1]
    assert n % bm == 0 and kk % bk == 0
    return pl.pallas_call(
        functools.partial(_mm_res_ln_kernel, alpha=alpha),
        grid=(n // bm, kk // bk),
        in_specs=[pl.BlockSpec((bm, bk), lambda i, k: (i, k)),
                  pl.BlockSpec((bk, d), lambda i, k: (k, 0)),
                  pl.BlockSpec((bm, d), lambda i, k: (i, 0)),
                  pl.BlockSpec((1, d), lambda i, k: (0, 0)),
                  pl.BlockSpec((1, d), lambda i, k: (0, 0))],
        out_specs=[pl.BlockSpec((bm, d), lambda i, k: (i, 0)),
                   pl.BlockSpec((bm, d), lambda i, k: (i, 0))],
        out_shape=[jax.ShapeDtypeStruct((n, d), F32), jax.ShapeDtypeStruct((n, d), BF16)],
        compiler_params=_params(("parallel", "arbitrary")),
        name="mm_res_ln",
    )(a, w, x, g.reshape(1, d), b.reshape(1, d))


def _swiglu(x, wg, wu):
    g = jnp.dot(x, wg, preferred_element_type=F32)
    u = jnp.dot(x, wu, preferred_element_type=F32)
    return g / (1.0 + jnp.exp(-g)) * u


def _swiglu_kernel(x_ref, wg_ref, wu_ref, o_ref):
    o_ref[...] = _swiglu(x_ref[...], wg_ref[...], wu_ref[...]).astype(o_ref.dtype)


def _swiglu_hidden(xb, wg, wu):
    n, d = xb.shape
    f = wg.shape[1]
    bm = _pick(n, 1024)
    bn = _pick(f, 512)
    return pl.pallas_call(
        _swiglu_kernel,
        grid=(n // bm, f // bn),
        in_specs=[pl.BlockSpec((bm, d), lambda i, j: (i, 0)),
                  pl.BlockSpec((d, bn), lambda i, j: (0, j)),
                  pl.BlockSpec((d, bn), lambda i, j: (0, j))],
        out_specs=pl.BlockSpec((bm, bn), lambda i, j: (i, j)),
        out_shape=jax.ShapeDtypeStruct((n, f), BF16),
        compiler_params=_params(("parallel", "arbitrary")),
        name="swiglu_hidden",
    )(xb, wg, wu)


def _router_kernel(x_ref, w_ref, comb_ref, *, n_experts):
    logits = jnp.dot(x_ref[...], w_ref[...], preferred_element_type=F32)
    lane = lax.broadcasted_iota(I32, logits.shape, 1).astype(F32)
    logits = jnp.where(lane < n_experts, logits, -jnp.inf)
    v1 = jnp.max(logits, axis=1, keepdims=True)
    i1 = jnp.min(jnp.where(logits == v1, lane, float(LANES)), axis=1, keepdims=True)
    rest = jnp.where(lane == i1, -jnp.inf, logits)
    v2 = jnp.max(rest, axis=1, keepdims=True)
    i2 = jnp.min(jnp.where(rest == v2, lane, float(LANES)), axis=1, keepdims=True)
    e2 = jnp.exp(v2 - v1)
    g1 = 1.0 / (1.0 + e2)
    g2 = e2 / (1.0 + e2)
    comb_ref[...] = jnp.where(lane == 0.0, i1, jnp.where(lane == 1.0, i2, jnp.where(
        lane == 2.0, g1, jnp.where(lane == 3.0, g2, 0.0))))


def _router(xb, rw, *, n_experts):
    n, d = xb.shape
    bm = _pick(n, 1024)
    return pl.pallas_call(
        functools.partial(_router_kernel, n_experts=n_experts),
        grid=(n // bm,),
        in_specs=[pl.BlockSpec((bm, d), lambda i: (i, 0)),
                  pl.BlockSpec((d, LANES), lambda i: (0, 0))],
        out_specs=pl.BlockSpec((bm, LANES), lambda i: (i, 0)),
        out_shape=jax.ShapeDtypeStruct((n, LANES), F32),
        compiler_params=_params(("parallel",)),
        name="router",
    )(xb, rw)


def _row_gather_kernel(idx_ref, src_ref, dst_ref, sem, *, rows, window):

    def row_copy(r):
        return pltpu.make_async_copy(src_ref.at[pl.ds(idx_ref[r], 1)],
                                     dst_ref.at[pl.ds(r, 1)], sem)

    def prime(r, carry):
        row_copy(r).start()
        return carry

    def steady(r, carry):
        row_copy(r - window).wait()
        row_copy(r).start()
        return carry

    def drain(r, carry):
        row_copy(r).wait()
        return carry

    lax.fori_loop(0, window, prime, 0, unroll=8)
    lax.fori_loop(window, rows, steady, 0, unroll=8)
    lax.fori_loop(rows - window, rows, drain, 0, unroll=8)


def _row_gather(src, idx):
    n_out = idx.shape[0]
    rows = _pick(n_out, 512)
    window = 256
    assert rows >= window and rows % 8 == 0 and window % 8 == 0
    return pl.pallas_call(
        functools.partial(_row_gather_kernel, rows=rows, window=window),
        grid=(n_out // rows,),
        in_specs=[pl.BlockSpec((rows,), lambda i: (i,), memory_space=pltpu.SMEM),
                  pl.BlockSpec(memory_space=pl.ANY)],
        out_specs=pl.BlockSpec((rows,) + src.shape[1:], lambda i: (i, 0)),
        out_shape=jax.ShapeDtypeStruct((n_out,) + src.shape[1:], src.dtype),
        scratch_shapes=[pltpu.SemaphoreType.DMA(())],
        compiler_params=_params(("arbitrary",)),
        name="row_gather",
    )(idx, src)


def _moe_up_kernel(te_ref, tv_ref, x_ref, wg_ref, wu_ref, o_ref):
    t = pl.program_id(1)

    @pl.when(tv_ref[t] != 0)
    def _():
        o_ref[...] = _swiglu(x_ref[...].astype(BF16), wg_ref[...], wu_ref[...]).astype(o_ref.dtype)

    @pl.when(tv_ref[t] == 0)
    def _():
        o_ref[...] = jnp.zeros(o_ref.shape, o_ref.dtype)


def _moe_up(xs, wg, wu, tile_expert, tile_valid, *, bm):
    n_rows, d = xs.shape
    f = wg.shape[2]
    bn = f // 4 if (f // 4) % LANES == 0 else _pick(f, 1024)
    grid_spec = pltpu.PrefetchScalarGridSpec(
        num_scalar_prefetch=2,
        grid=(f // bn, n_rows // bm),
        in_specs=[pl.BlockSpec((bm, d), lambda j, t, te, tv: (t, 0)),
                  pl.BlockSpec((None, d, bn), lambda j, t, te, tv: (te[t], 0, j)),
                  pl.BlockSpec((None, d, bn), lambda j, t, te, tv: (te[t], 0, j))],
        out_specs=pl.BlockSpec((bm, bn), lambda j, t, te, tv: (t, j)))
    return pl.pallas_call(
        _moe_up_kernel,
        grid_spec=grid_spec,
        out_shape=jax.ShapeDtypeStruct((n_rows, f), BF16),
        compiler_params=_params(("arbitrary", "arbitrary")),
        name="moe_up",
    )(tile_expert, tile_valid, xs, wg, wu)


def _moe_down_kernel(te_ref, tv_ref, h_ref, w_ref, o_ref):
    t = pl.program_id(0)

    @pl.when(pl.program_id(1) == 0)
    def _():
        o_ref[...] = jnp.zeros(o_ref.shape, o_ref.dtype)

    @pl.when(tv_ref[t] != 0)
    def _():
        o_ref[...] += jnp.dot(h_ref[...], w_ref[...], preferred_element_type=F32)


def _moe_down(hs, wd, tile_expert, tile_valid, *, bm):
    n_rows, f = hs.shape
    d = wd.shape[2]
    bk = f // 2 if (f // 2) % LANES == 0 else f
    grid_spec = pltpu.PrefetchScalarGridSpec(
        num_scalar_prefetch=2,
        grid=(n_rows // bm, f // bk),
        in_specs=[pl.BlockSpec((bm, bk), lambda t, k, te, tv: (t, k)),
                  pl.BlockSpec((None, bk, d), lambda t, k, te, tv: (te[t], k, 0))],
        out_specs=pl.BlockSpec((bm, d), lambda t, k, te, tv: (t, 0)))
    return pl.pallas_call(
        _moe_down_kernel,
        grid_spec=grid_spec,
        out_shape=jax.ShapeDtypeStruct((n_rows, d), F32),
        compiler_params=_params(("arbitrary", "arbitrary")),
        name="moe_down",
    )(tile_expert, tile_valid, hs, wd)


def _moe_combine_kernel(y1_ref, y2_ref, r_ref, x_ref, g_ref, b_ref, of_ref, ob_ref, *, alpha):
    route = r_ref[...]
    y = route[:, 2:3] * y1_ref[...] + route[:, 3:4] * y2_ref[...]
    out = _layer_norm(alpha * x_ref[...] + y, g_ref[...], b_ref[...])
    of_ref[...] = out
    ob_ref[...] = out.astype(BF16)


def _moe_combine(y_both, route, x, g, b, *, alpha):
    n, d = x.shape
    bm = _pick(n, 256)
    nb = n // bm
    return pl.pallas_call(
        functools.partial(_moe_combine_kernel, alpha=alpha),
        grid=(nb,),
        in_specs=[pl.BlockSpec((bm, d), lambda i: (i, 0)),
                  pl.BlockSpec((bm, d), lambda i: (i + nb, 0)),
                  pl.BlockSpec((bm, LANES), lambda i: (i, 0)),
                  pl.BlockSpec((bm, d), lambda i: (i, 0)),
                  pl.BlockSpec((1, d), lambda i: (0, 0)),
                  pl.BlockSpec((1, d), lambda i: (0, 0))],
        out_specs=[pl.BlockSpec((bm, d), lambda i: (i, 0)),
                   pl.BlockSpec((bm, d), lambda i: (i, 0))],
        out_shape=[jax.ShapeDtypeStruct((n, d), F32), jax.ShapeDtypeStruct((n, d), BF16)],
        compiler_params=_params(("parallel",)),
        name="moe_combine",
    )(y_both, y_both, route, x, g.reshape(1, d), b.reshape(1, d))


def _route_plan(route, n_experts, bm):
    n = route.shape[0]
    n_assign = TOP_K_EXPERTS * n
    n_rows = n_assign + n_experts * bm
    flat_e = route[:, :TOP_K_EXPERTS].astype(I32).reshape(n_assign)
    onehot = (flat_e[:, None] == jnp.arange(n_experts, dtype=I32)[None, :]).astype(I32)
    csum = jnp.cumsum(onehot, axis=0)
    rank = jnp.sum(onehot * csum, axis=1) - 1
    counts = csum[-1]
    padded = (counts + bm - 1) // bm * bm
    ends = jnp.cumsum(padded)
    starts = ends - padded
    dest = jnp.sum(onehot * starts[None, :], axis=1) + rank
    src = jnp.zeros((n_rows,), I32).at[dest].set(jnp.arange(n_assign, dtype=I32) // TOP_K_EXPERTS)
    tile_start = jnp.arange(n_rows // bm, dtype=I32) * bm
    tile_expert = jnp.minimum(jnp.sum((tile_start[:, None] >= ends[None, :]).astype(I32), axis=1),
                              n_experts - 1)
    tile_valid = (tile_start < ends[-1]).astype(I32)
    return src, dest, tile_expert, tile_valid


def _rope_tables(positions):
    pos = positions.astype(F32)[..., None]
    inv_h = ROPE_THETA ** (-jnp.arange(0, HEAD_DIM, 2, dtype=F32) / HEAD_DIM)
    inv_i = ROPE_THETA ** (-jnp.arange(0, IDX_DIM, 2, dtype=F32) / IDX_DIM)
    ang_h = pos * inv_h
    ang_i = pos * inv_i
    ch, sh = jnp.cos(ang_h), jnp.sin(ang_h)
    ci, si = jnp.cos(ang_i), jnp.sin(ang_i)
    z32 = jnp.zeros_like(si)
    one64 = jnp.ones(ci.shape[:-1] + (IDX_DIM,), F32)
    z64 = jnp.zeros_like(one64)
    cat = lambda *a: jnp.concatenate(a, axis=-1).reshape(-1, LANES)
    main = (cat(ch, ch), cat(-sh, sh), cat(ci, ci, ci, ci), cat(-si, z32, -si, z32),
            cat(z32, si, z32, si))
    small = (main[0], main[1], cat(ci, ci, one64), cat(-si, z32, z64), cat(z32, si, z64))
    return main, small


def kernel(x, positions, ln_in_g, ln_in_b, w_in, w_proj_a, w_proj_b, w_out, ln_mix_g, ln_mix_b,
           ffn_w_gate, ffn_w_up, ffn_w_down, router_w, moe_w_gate, moe_w_up, moe_w_down,
           ln_ffn_g, ln_ffn_b):
    bsz, s_len, d = x.shape
    depth = w_in.shape[0]
    n = bsz * s_len
    n_heads = d // (2 * HEAD_DIM)
    d_a = n_heads * HEAD_DIM
    d_qi = IDX_HEADS * IDX_DIM
    n_experts = router_w.shape[-1]
    alpha = (2.0 * depth) ** 0.25
    n_sel = min(TOPK_MAX, s_len // 4)
    tq = 256
    assert s_len % tq == 0 and tq % CHUNK == 0 and n_sel <= tq and d_qi % d_a == 0

    off_ki = 3 * d_a + d_qi
    off_qb = off_ki + IDX_DIM + IDX_HEADS
    qb_col_block = (3 * d_a + d_qi) // d_a
    ga_col = 6 * d_a + d_qi

    tabs_main, tabs_small = _rope_tables(positions)
    tri = (lax.broadcasted_iota(I32, (tq, tq), 0) >= lax.broadcasted_iota(I32, (tq, tq), 1)
           ).astype(BF16)

    xf, xb = _input_ln(x.reshape(n, d), ln_in_g, ln_in_b)
    for layer in range(depth):
        w_l = w_in[layer]
        w_main = jnp.concatenate([w_l[:, :off_ki], w_l[:, off_qb:]], axis=1).astype(BF16)
        w_small = jnp.pad(w_l[:, off_ki:off_qb],
                          ((0, 0), (0, LANES - IDX_DIM - IDX_HEADS))).astype(BF16)
        hmain = _inproj(xb, w_main, tabs_main, bn=d_a, rope128_tiles=2,
                        rope64_tiles=(3, 3 + d_qi // d_a), stick_q_tile=qb_col_block,
                        out_dtype=BF16)
        hsmall = _inproj(xb, w_small, tabs_small, bn=LANES, rope128_tiles=0,
                         rope64_tiles=(0, 1), stick_q_tile=-1, out_dtype=F32)
        h3 = hmain.reshape(bsz, s_len, -1)
        hs3 = hsmall.reshape(bsz, s_len, LANES)

        qat = h3[..., :d_a].reshape(bsz, s_len, n_heads, HEAD_DIM).transpose(0, 2, 3, 1)
        vat = (h3[..., 2 * d_a:3 * d_a].reshape(bsz, s_len // tq, tq, n_heads, HEAD_DIM)
               .transpose(0, 3, 1, 4, 2))
        vat = jnp.concatenate(
            [vat, jnp.ones((bsz, n_heads, s_len // tq, ONES_ROWS, tq), BF16)], axis=3)
        qit = (h3[..., 3 * d_a:3 * d_a + d_qi].reshape(bsz, s_len, IDX_HEADS, IDX_DIM)
               .transpose(0, 2, 3, 1))
        ki = hs3[..., :IDX_DIM].astype(BF16)
        wit = hs3[..., IDX_DIM:IDX_DIM + IDX_HEADS].transpose(0, 2, 1)

        bias = _indexer(qit, ki, wit, tri, tq=tq, n_sel=n_sel)
        oat = _sparse_attn(qat, h3, vat, bias, tq=tq, n_heads=n_heads, k_col_block=1)
        o_a = oat.transpose(0, 3, 1, 2).reshape(n, d_a)
        o_b = _stick_attn(h3, tri, tq=tq, n_heads=n_heads, q_col_block=qb_col_block
                          ).reshape(n, d_a)

        merged = _gated_proj(o_a, o_b, w_proj_a[layer].astype(BF16), w_proj_b[layer].astype(BF16),
                             hmain, ga_col=ga_col)
        xf, xb = _mm_res_ln(merged, w_out[layer].astype(BF16), xf, ln_mix_g[layer],
                            ln_mix_b[layer], alpha=alpha, bm=_pick(n, 512), bk=d)

        i = layer // 2
        if layer % 2 == 0:
            hid = _swiglu_hidden(xb, ffn_w_gate[i].astype(BF16), ffn_w_up[i].astype(BF16))
            f_dim = hid.shape[1]
            bk = f_dim // 2 if (f_dim // 2) % LANES == 0 else f_dim
            xf, xb = _mm_res_ln(hid, ffn_w_down[i].astype(BF16), xf, ln_ffn_g[layer],
                                ln_ffn_b[layer], alpha=alpha, bm=_pick(n, 512), bk=bk)
        else:
            rw = jnp.pad(router_w[i], ((0, 0), (0, LANES - n_experts))).astype(BF16)
            route = _router(xb, rw, n_experts=n_experts)
            bm_e = _pick(n, 512)
            src, dest, tile_expert, tile_valid = _route_plan(route, n_experts, bm_e)
            xs = _row_gather(xf, src)
            hs = _moe_up(xs, moe_w_gate[i].astype(BF16), moe_w_up[i].astype(BF16),
                         tile_expert, tile_valid, bm=bm_e)
            ys = _moe_down(hs, moe_w_down[i].astype(BF16), tile_expert, tile_valid, bm=bm_e)
            y_both = _row_gather(ys, dest.reshape(n, TOP_K_EXPERTS).T.reshape(-1))
            xf, xb = _moe_combine(y_both, route, xf, ln_ffn_g[layer], ln_ffn_b[layer],
                                  alpha=alpha)
    return xf.reshape(bsz, s_len, d)
```

```python
import functools
import math

import jax
import jax.numpy as jnp
from jax import lax
from jax.experimental import pallas as pl
from jax.experimental.pallas import tpu as pltpu

F32 = jnp.float32
BF16 = jnp.bfloat16
I32 = jnp.int32
I16 = jnp.int16

LN_EPS = 1e-5
HEAD_DIM = 128
IDX_HEADS = 16
IDX_DIM = 64
CHUNK = 64
TOPK_MAX = 256
ROPE_THETA = 10000.0
TOP_K_EXPERTS = 2
LANES = 128
VMEM_LIMIT = 56 * 1024 * 1024
MASK_NEG = -1e30
KEY_NEG_INF = -2139095041
STICK_SKIP_LOG = -100.0
SPARSE_Q_SCALE = math.log2(math.e) / math.sqrt(HEAD_DIM)
ONES_ROWS = 16
INT_MIN = -2147483648


def _params(sem):
    return pltpu.CompilerParams(dimension_semantics=sem, vmem_limit_bytes=VMEM_LIMIT)


def _pick(n, pref):
    b = min(n, pref)
    while n % b:
        b //= 2
    return b


def _layer_norm(v, g, b):
    mu = jnp.mean(v, axis=-1, keepdims=True)
    c = v - mu
    var = jnp.mean(c * c, axis=-1, keepdims=True)
    return c * lax.rsqrt(var + LN_EPS) * g + b


def _ln_kernel(x_ref, g_ref, b_ref, of_ref, ob_ref):
    y = _layer_norm(x_ref[...], g_ref[...], b_ref[...])
    of_ref[...] = y
    ob_ref[...] = y.astype(BF16)


def _input_ln(x2d, g, b):
    n, d = x2d.shape
    bm = _pick(n, 512)
    return pl.pallas_call(
        _ln_kernel,
        grid=(n // bm,),
        in_specs=[pl.BlockSpec((bm, d), lambda i: (i, 0)),
                  pl.BlockSpec((1, d), lambda i: (0, 0)),
                  pl.BlockSpec((1, d), lambda i: (0, 0))],
        out_specs=[pl.BlockSpec((bm, d), lambda i: (i, 0)),
                   pl.BlockSpec((bm, d), lambda i: (i, 0))],
        out_shape=[jax.ShapeDtypeStruct((n, d), F32), jax.ShapeDtypeStruct((n, d), BF16)],
        compiler_params=_params(("parallel",)),
        name="input_ln",
    )(x2d, g.reshape(1, d), b.reshape(1, d))


def _inproj_kernel(x_ref, w_ref, c128_ref, s128_ref, c64_ref, sa64_ref, sb64_ref, o_ref, *,
                   rope128_tiles, rope64_tiles, stick_q_tile):
    j = pl.program_id(1)
    n_heads = o_ref.shape[1] // LANES
    lo64, hi64 = rope64_tiles
    is128 = j < rope128_tiles
    is64 = jnp.logical_and(j >= lo64, j < hi64)

    def project():
        return jnp.dot(x_ref[...], w_ref[...], preferred_element_type=F32)

    @pl.when(is128)
    def _():
        q_scale = jnp.where(j == 0, SPARSE_Q_SCALE, 1.0)
        c = c128_ref[...] * q_scale
        s = s128_ref[...] * q_scale
        y = project()
        for h in range(n_heads):
            yh = y[:, h * LANES:(h + 1) * LANES]
            o_ref[:, h * LANES:(h + 1) * LANES] = (
                yh * c + pltpu.roll(yh, HEAD_DIM // 2, 1) * s).astype(o_ref.dtype)

    @pl.when(is64)
    def _():
        c = c64_ref[...]
        sa = sa64_ref[...]
        sb = sb64_ref[...]
        y = project()
        for h in range(n_heads):
            yh = y[:, h * LANES:(h + 1) * LANES]
            o_ref[:, h * LANES:(h + 1) * LANES] = (
                yh * c + pltpu.roll(yh, LANES - IDX_DIM // 2, 1) * sa
                + pltpu.roll(yh, IDX_DIM // 2, 1) * sb).astype(o_ref.dtype)

    @pl.when(jnp.logical_not(jnp.logical_or(is128, is64)))
    def _():
        o_ref[...] = (project() * jnp.where(j == stick_q_tile, 1.0 / math.sqrt(HEAD_DIM), 1.0)
                      ).astype(o_ref.dtype)


def _inproj(xb, w, tabs, *, bn, rope128_tiles, rope64_tiles, stick_q_tile, out_dtype):
    n, d = xb.shape
    cols = w.shape[1]
    bm = _pick(n, 1024)
    tab_spec = pl.BlockSpec((bm, LANES), lambda i, j: (i, 0))
    return pl.pallas_call(
        functools.partial(_inproj_kernel, rope128_tiles=rope128_tiles, rope64_tiles=rope64_tiles,
                          stick_q_tile=stick_q_tile),
        grid=(n // bm, cols // bn),
        in_specs=[pl.BlockSpec((bm, d), lambda i, j: (i, 0)),
                  pl.BlockSpec((d, bn), lambda i, j: (0, j))] + [tab_spec] * 5,
        out_specs=pl.BlockSpec((bm, bn), lambda i, j: (i, j)),
        out_shape=jax.ShapeDtypeStruct((n, cols), out_dtype),
        compiler_params=_params(("parallel", "arbitrary")),
        name="inproj",
    )(xb, w, *tabs)


def _indexer_kernel(qit_ref, ki_ref, wit_ref, tri_ref, bias_ref, skey_ref, hi_ref, lo_ref, *, tq,
                    n_sel):
    qb = pl.program_id(1)
    n_kt = bias_ref.shape[0] // tq
    q_chunk = (qb * tq + lax.broadcasted_iota(I32, (tq, tq), 1)) // CHUNK
    k_local = lax.broadcasted_iota(I32, (tq, tq), 0)

    def score_tile(kt):
        k0 = pl.multiple_of(kt * tq, tq)
        k_t = ki_ref[pl.ds(k0, tq), :]
        acc = jnp.zeros((tq, tq), F32)
        for h in range(IDX_HEADS):
            lg = jnp.dot(k_t, qit_ref[h], preferred_element_type=F32)
            acc = acc + jnp.maximum(lg, 0.0) * wit_ref[h:h + 1, :]
        acc = jnp.where((k0 + k_local) // CHUNK <= q_chunk, acc, -jnp.inf)
        bits = pltpu.bitcast(acc, I32)
        key = bits ^ ((bits >> 31) & 0x7FFFFFFF)
        skey_ref[pl.ds(k0, tq), :] = key
        hi_ref[pl.ds(k0, tq), :] = (key >> 16).astype(I16)
        lo_ref[pl.ds(k0, tq), :] = ((key & 0xFFFF) - 32768).astype(I16)

    odd = (qb + 1) % 2

    @pl.when(odd == 1)
    def _():
        score_tile(0)

    def score_pair(i, carry):
        score_tile(odd + 2 * i)
        score_tile(odd + 2 * i + 1)
        return carry

    lax.fori_loop(0, (qb + 1) // 2, score_pair, 0)

    def count_ge(cand):
        def body(kt, accs):
            k0 = pl.multiple_of(kt * tq, tq)
            accs = list(accs)
            x = skey_ref[pl.ds(k0, tq), :]
            for r in range(tq // 8):
                m = x[r * 8:(r + 1) * 8, :] >= cand
                a = accs[r % len(accs)]
                accs[r % len(accs)] = jnp.where(m, a + 1, a)
            return tuple(accs)
        accs = lax.fori_loop(0, qb + 1, body, tuple(jnp.zeros((8, tq), I32) for _ in range(4)))
        return (accs[0] + accs[1] + accs[2] + accs[3]).sum(axis=0, keepdims=True)

    def count_ge16(ref, cand):
        cand16 = cand.astype(I16)

        def body(kt, accs):
            k0 = pl.multiple_of(kt * tq, tq)
            accs = list(accs)
            x = ref[pl.ds(k0, tq), :]
            for r in range(tq // 16):
                m = x[r * 16:(r + 1) * 16, :] >= cand16
                a = accs[r % len(accs)]
                accs[r % len(accs)] = jnp.where(m, a + 1, a)
            return tuple(accs)
        accs = lax.fori_loop(0, qb + 1, body, tuple(jnp.zeros((16, tq), I16) for _ in range(4)))
        return sum(a.astype(I32) for a in accs).sum(axis=0, keepdims=True)

    def search16(ref, want, n_steps, cnt_all):
        def step(i, carry):
            ans, cnt = carry
            cand = ans + lax.shift_left(jnp.int32(1), 15 - i)
            cnt_cand = count_ge16(ref, cand)
            keep = cnt_cand >= want
            return jnp.where(keep, cand, ans), jnp.where(keep, cnt_cand, cnt)
        return lax.fori_loop(0, n_steps, step, (jnp.full((1, tq), -32768, I32), cnt_all))

    n_steps = jnp.where((qb + 1) * tq > n_sel, 16, 0)
    t_hi, cnt_hi = search16(hi_ref, n_sel, n_steps, jnp.full((1, tq), (qb + 1) * tq, I32))
    above = count_ge16(hi_ref, t_hi + 1)
    t_hi16 = t_hi.astype(I16)

    def bucket_tile(kt, carry):
        k0 = pl.multiple_of(kt * tq, tq)
        lo_ref[pl.ds(k0, tq), :] = jnp.where(hi_ref[pl.ds(k0, tq), :] == t_hi16,
                                             lo_ref[pl.ds(k0, tq), :], jnp.int16(-32768))
        return carry

    lax.fori_loop(0, jnp.where(n_steps > 0, qb + 1, 0), bucket_tile, 0)
    t_lo, cnt_lo = search16(lo_ref, n_sel - above, n_steps, cnt_hi - above)
    searched = n_steps > 0
    thr = jnp.where(searched, t_hi * 65536 + (t_lo + 32768), INT_MIN)
    cnt = jnp.where(searched, above + cnt_lo, 0)
    tied = jnp.where(thr > KEY_NEG_INF, jnp.where(cnt > n_sel, 1, 0), 0)
    any_tied = jnp.max(tied) > 0
    thr = jnp.maximum(thr, KEY_NEG_INF + 1)

    @pl.when(jnp.logical_not(any_tied))
    def _():
        def bias_tile(kt, carry):
            k0 = pl.multiple_of(kt * tq, tq)
            sel = skey_ref[pl.ds(k0, tq), :] >= thr
            bias_ref[pl.ds(k0, tq), :] = jnp.where(sel, 0.0, MASK_NEG).astype(BF16)
            return carry

        lax.fori_loop(0, qb + 1, bias_tile, 0)

    @pl.when(any_tied)
    def _():
        need = (n_sel - count_ge(thr + 1)).astype(F32)
        tri = tri_ref[...]

        def bias_tile(kt, seen):
            k0 = pl.multiple_of(kt * tq, tq)
            x = skey_ref[pl.ds(k0, tq), :]
            eq = x == thr
            rank = seen + jnp.dot(tri, jnp.where(eq, 1.0, 0.0).astype(BF16),
                                  preferred_element_type=F32)
            tie_bias = jnp.where(eq, jnp.where(rank <= need, 0.0, MASK_NEG), MASK_NEG)
            bias_ref[pl.ds(k0, tq), :] = jnp.where(x > thr, 0.0, tie_bias).astype(BF16)
            return rank[tq - 1:tq, :]

        lax.fori_loop(0, qb + 1, bias_tile, jnp.zeros((1, tq), F32))

    def fill_tile(kt, carry):
        k0 = pl.multiple_of(kt * tq, tq)
        bias_ref[pl.ds(k0, tq), :] = jnp.full((tq, tq), MASK_NEG, BF16)
        return carry

    lax.fori_loop(qb + 1, n_kt, fill_tile, 0)


def _indexer(qit, ki, wit, tri, *, tq, n_sel):
    bsz, _, _, s_len = qit.shape
    return pl.pallas_call(
        functools.partial(_indexer_kernel, tq=tq, n_sel=n_sel),
        grid=(bsz, s_len // tq),
        in_specs=[pl.BlockSpec((None, IDX_HEADS, IDX_DIM, tq), lambda b, q: (b, 0, 0, q)),
                  pl.BlockSpec((None, s_len, IDX_DIM), lambda b, q: (b, 0, 0)),
                  pl.BlockSpec((None, IDX_HEADS, tq), lambda b, q: (b, 0, q)),
                  pl.BlockSpec((tq, tq), lambda b, q: (0, 0))],
        out_specs=pl.BlockSpec((None, s_len, tq), lambda b, q: (b, 0, q)),
        out_shape=jax.ShapeDtypeStruct((bsz, s_len, s_len), BF16),
        scratch_shapes=[pltpu.VMEM((s_len, tq), I32), pltpu.VMEM((s_len, tq), I16),
                        pltpu.VMEM((s_len, tq), I16)],
        compiler_params=_params(("parallel", "arbitrary")),
        name="indexer_topk",
    )(qit, ki, wit, tri)


def _sparse_attn_kernel(qt_ref, k_ref, vt_ref, bias_ref, ot_ref, acc_ref, s_ref, p_ref, *, tq,
                        n_heads):
    qb = pl.program_id(1)
    acc_ref[...] = jnp.zeros(acc_ref.shape, F32)

    def tile(kt, ms, slot):
        k0 = pl.multiple_of(kt * tq, tq)
        for h in range(n_heads):
            k_t = k_ref[pl.ds(k0, tq), h * HEAD_DIM:(h + 1) * HEAD_DIM]
            s_ref[slot, h] = jnp.dot(k_t, qt_ref[h], preferred_element_type=F32).astype(BF16)
        bias = bias_ref[pl.ds(k0, tq), :]
        new_ms, alphas = [], []
        for h in range(n_heads):
            s = s_ref[slot, h] + bias
            m_new = jnp.maximum(ms[h], s.max(axis=0, keepdims=True).astype(F32))
            alphas.append(jnp.exp2(ms[h] - m_new))
            p_ref[slot, h] = jnp.exp2(s - m_new.astype(BF16))
            new_ms.append(m_new)
        for h in range(n_heads):
            acc_ref[h] = acc_ref[h] * alphas[h] + jnp.dot(vt_ref[h, kt], p_ref[slot, h],
                                                          preferred_element_type=F32)
        return tuple(new_ms)

    odd = (qb + 1) % 2
    ms = tuple(jnp.full((1, tq), -jnp.inf, F32) for _ in range(n_heads))
    ms = lax.cond(odd == 1, lambda m: tile(0, m, 0), lambda m: m, ms)

    def pair(i, ms):
        kt = odd + 2 * i
        return tile(kt + 1, tile(kt, ms, 0), 1)

    lax.fori_loop(0, (qb + 1) // 2, pair, ms)
    for h in range(n_heads):
        ot_ref[h] = (acc_ref[h, :HEAD_DIM, :] / acc_ref[h, HEAD_DIM:HEAD_DIM + 1, :]
                     ).astype(ot_ref.dtype)


def _sparse_attn(qat, h3, vat, bias, *, tq, n_heads, k_col_block):
    bsz, s_len, _ = h3.shape
    d_a = n_heads * HEAD_DIM
    n_kt = s_len // tq
    v_rows = HEAD_DIM + ONES_ROWS
    assert vat.shape == (bsz, n_heads, n_kt, v_rows, tq)
    return pl.pallas_call(
        functools.partial(_sparse_attn_kernel, tq=tq, n_heads=n_heads),
        grid=(bsz, s_len // tq),
        in_specs=[pl.BlockSpec((None, n_heads, HEAD_DIM, tq), lambda b, q: (b, 0, 0, q)),
                  pl.BlockSpec((None, s_len, d_a), lambda b, q: (b, 0, k_col_block)),
                  pl.BlockSpec((None, n_heads, n_kt, v_rows, tq), lambda b, q: (b, 0, 0, 0, 0)),
                  pl.BlockSpec((None, s_len, tq), lambda b, q: (b, 0, q))],
        out_specs=pl.BlockSpec((None, n_heads, HEAD_DIM, tq), lambda b, q: (b, 0, 0, q)),
        out_shape=jax.ShapeDtypeStruct((bsz, n_heads, HEAD_DIM, s_len), BF16),
        scratch_shapes=[pltpu.VMEM((n_heads, v_rows, tq), F32),
                        pltpu.VMEM((2, n_heads, tq, tq), BF16),
                        pltpu.VMEM((2, n_heads, tq, tq), BF16)],
        compiler_params=_params(("parallel", "arbitrary")),
        name="sparse_attn",
    )(qat, h3, vat, bias)


def _stick_kernel(q_ref, k_ref, v_ref, tri_ref, o_ref, kmax_ref, acc_ref, run_ref, z_ref, suf_ref,
                  hi_ref, lo_ref, a_ref, *, tq, n_heads):
    qb = pl.program_id(1)
    s_len, d_b = k_ref.shape
    heads = [slice(h * HEAD_DIM, (h + 1) * HEAD_DIM) for h in range(n_heads)]

    @pl.when(qb == 0)
    def _():
        def body(kt, m8):
            k_t = jnp.abs(k_ref[pl.ds(pl.multiple_of(kt * tq, tq), tq), :].astype(F32))
            return jnp.maximum(m8, k_t.reshape(tq // 8, 8, d_b).max(axis=0))
        m8 = lax.fori_loop(0, s_len // tq, body, jnp.zeros((8, d_b), F32))
        kmax_ref[...] = m8.max(axis=0, keepdims=True)

    z_bounds = [jnp.sum(jnp.abs(q_ref[:, c].astype(F32)) * kmax_ref[:, c], axis=1, keepdims=True)
                for c in heads]
    tri = tri_ref[...]

    def tile_step(k0, strict):
        for h, c in enumerate(heads):
            z_ref[h] = lax.dot_general(q_ref[:, c], k_ref[pl.ds(k0, tq), c],
                                       (((1,), (1,)), ((), ())), preferred_element_type=F32)
        for h in range(n_heads):
            z = z_ref[h]
            log_keep = -(jnp.maximum(z, 0.0) + jnp.log(1.0 + jnp.exp(-jnp.abs(z))))
            if strict is not None:
                log_keep = jnp.where(strict, log_keep, 0.0)
            hi = log_keep.astype(BF16)
            hi_ref[h] = hi
            lo_ref[h] = (log_keep - hi.astype(F32)).astype(BF16)
        for h in range(n_heads):
            suf_ref[h] = (jnp.dot(hi_ref[h], tri, preferred_element_type=F32)
                          + jnp.dot(lo_ref[h], tri, preferred_element_type=F32))
        bound = None
        for h in range(n_heads):
            suffix = suf_ref[h]
            a = jnp.exp(z_ref[h] + suffix + run_ref[h])
            if strict is not None:
                a = jnp.where(strict, a, 0.0)
            a_ref[h] = a.astype(BF16)
            run = run_ref[h] + suffix[:, 0:1]
            run_ref[h] = run
            b_h = jnp.max(run + z_bounds[h])
            bound = b_h if bound is None else jnp.maximum(bound, b_h)
        for h, c in enumerate(heads):
            acc_ref[:, c] += jnp.dot(a_ref[h], v_ref[pl.ds(k0, tq), c],
                                     preferred_element_type=F32)
        return bound

    acc_ref[...] = jnp.zeros(acc_ref.shape, F32)
    run_ref[...] = jnp.zeros(run_ref.shape, F32)
    strict = lax.broadcasted_iota(I32, (tq, tq), 1) < lax.broadcasted_iota(I32, (tq, tq), 0)
    bound0 = tile_step(pl.multiple_of(qb * tq, tq), strict)

    def cond(carry):
        i, bound = carry
        return jnp.logical_and(i <= qb, bound > STICK_SKIP_LOG)

    def body(carry):
        i, _ = carry
        return i + 1, tile_step(pl.multiple_of((qb - i) * tq, tq), None)

    lax.while_loop(cond, body, (jnp.int32(1), bound0))
    o_ref[...] = acc_ref[...].astype(o_ref.dtype)


def _stick_attn(h3, tri, *, tq, n_heads, q_col_block):
    bsz, s_len, _ = h3.shape
    d_b = n_heads * HEAD_DIM
    return pl.pallas_call(
        functools.partial(_stick_kernel, tq=tq, n_heads=n_heads),
        grid=(bsz, s_len // tq),
        in_specs=[pl.BlockSpec((None, tq, d_b), lambda b, q: (b, q, q_col_block)),
                  pl.BlockSpec((None, s_len, d_b), lambda b, q: (b, 0, q_col_block + 1)),
                  pl.BlockSpec((None, s_len, d_b), lambda b, q: (b, 0, q_col_block + 2)),
                  pl.BlockSpec((tq, tq), lambda b, q: (0, 0))],
        out_specs=pl.BlockSpec((None, tq, d_b), lambda b, q: (b, q, 0)),
        out_shape=jax.ShapeDtypeStruct((bsz, s_len, d_b), BF16),
        scratch_shapes=[pltpu.VMEM((1, d_b), F32),
                        pltpu.VMEM((tq, d_b), F32),
                        pltpu.VMEM((n_heads, tq, 1), F32),
                        pltpu.VMEM((n_heads, tq, tq), F32),
                        pltpu.VMEM((n_heads, tq, tq), F32),
                        pltpu.VMEM((n_heads, tq, tq), BF16),
                        pltpu.VMEM((n_heads, tq, tq), BF16),
                        pltpu.VMEM((n_heads, tq, tq), BF16)],
        compiler_params=_params(("arbitrary", "arbitrary")),
        name="stick_attn",
    )(h3, h3, h3, tri)


def _gated_proj_kernel(oa_ref, ob_ref, pa_ref, pb_ref, ga_ref, gb_ref, o_ref):
    ya = jnp.dot(oa_ref[...], pa_ref[...], preferred_element_type=F32)
    yb = jnp.dot(ob_ref[...], pb_ref[...], preferred_element_type=F32)
    sa = 1.0 / (1.0 + jnp.exp(-ga_ref[...].astype(F32)))
    sb = 1.0 / (1.0 + jnp.exp(-gb_ref[...].astype(F32)))
    o_ref[...] = (sa * ya + sb * yb).astype(o_ref.dtype)


def _gated_proj(oa, ob, pa, pb, hmain, *, ga_col):
    n, d_a = oa.shape
    d = pa.shape[1]
    bm = _pick(n, 1024)
    bn = _pick(d, 1024)
    ga_blk = ga_col // bn
    gb_blk = (ga_col + d) // bn
    return pl.pallas_call(
        _gated_proj_kernel,
        grid=(n // bm, d // bn),
        in_specs=[pl.BlockSpec((bm, d_a), lambda i, j: (i, 0)),
                  pl.BlockSpec((bm, d_a), lambda i, j: (i, 0)),
                  pl.BlockSpec((d_a, bn), lambda i, j: (0, j)),
                  pl.BlockSpec((d_a, bn), lambda i, j: (0, j)),
                  pl.BlockSpec((bm, bn), lambda i, j: (i, ga_blk + j)),
                  pl.BlockSpec((bm, bn), lambda i, j: (i, gb_blk + j))],
        out_specs=pl.BlockSpec((bm, bn), lambda i, j: (i, j)),
        out_shape=jax.ShapeDtypeStruct((n, d), BF16),
        compiler_params=_params(("parallel", "arbitrary")),
        name="gated_proj",
    )(oa, ob, pa, pb, hmain, hmain)


def _mm_res_ln_kernel(a_ref, w_ref, x_ref, g_ref, b_ref, of_ref, ob_ref, *, alpha):
    k = pl.program_id(1)

    @pl.when(k == 0)
    def _():
        of_ref[...] = jnp.dot(a_ref[...], w_ref[...], preferred_element_type=F32)

    @pl.when(k > 0)
    def _():
        of_ref[...] += jnp.dot(a_ref[...], w_ref[...], preferred_element_type=F32)

    @pl.when(k == pl.num_programs(1) - 1)
    def _():
        y = _layer_norm(alpha * x_ref[...] + of_ref[...], g_ref[...], b_ref[...])
        of_ref[...] = y
        ob_ref[...] = y.astype(BF16)


def _mm_res_ln(a, w, x, g, b, *, alpha, bm, bk):
    n, kk = a.shape
    d = w.shape[1]
    assert n % bm == 0 and kk % bk == 0
    return pl.pallas_call(
        functools.partial(_mm_res_ln_kernel, alpha=alpha),
        grid=(n // bm, kk // bk),
        in_specs=[pl.BlockSpec((bm, bk), lambda i, k: (i, k)),
                  pl.BlockSpec((bk, d), lambda i, k: (k, 0)),
                  pl.BlockSpec((bm, d), lambda i, k: (i, 0)),
                  pl.BlockSpec((1, d), lambda i, k: (0, 0)),
                  pl.BlockSpec((1, d), lambda i, k: (0, 0))],
        out_specs=[pl.BlockSpec((bm, d), lambda i, k: (i, 0)),
                   pl.BlockSpec((bm, d), lambda i, k: (i, 0))],
        out_shape=[jax.ShapeDtypeStruct((n, d), F32), jax.ShapeDtypeStruct((n, d), BF16)],
        compiler_params=_params(("parallel", "arbitrary")),
        name="mm_res_ln",
    )(a, w, x, g.reshape(1, d), b.reshape(1, d))


def _swiglu(x, wg, wu):
    g = jnp.dot(x, wg, preferred_element_type=F32)
    u = jnp.dot(x, wu, preferred_element_type=F32)
    return g / (1.0 + jnp.exp(-g)) * u


def _swiglu_kernel(x_ref, wg_ref, wu_ref, o_ref):
    o_ref[...] = _swiglu(x_ref[...], wg_ref[...], wu_ref[...]).astype(o_ref.dtype)


def _swiglu_hidden(xb, wg, wu):
    n, d = xb.shape
    f = wg.shape[1]
    bm = _pick(n, 1024)
    bn = _pick(f, 512)
    return pl.pallas_call(
        _swiglu_kernel,
        grid=(n // bm, f // bn),
        in_specs=[pl.BlockSpec((bm, d), lambda i, j: (i, 0)),
                  pl.BlockSpec((d, bn), lambda i, j: (0, j)),
                  pl.BlockSpec((d, bn), lambda i, j: (0, j))],
        out_specs=pl.BlockSpec((bm, bn), lambda i, j: (i, j)),
        out_shape=jax.ShapeDtypeStruct((n, f), BF16),
        compiler_params=_params(("parallel", "arbitrary")),
        name="swiglu_hidden",
    )(xb, wg, wu)


def _router_kernel(x_ref, w_ref, comb_ref, *, n_experts):
    logits = jnp.dot(x_ref[...], w_ref[...], preferred_element_type=F32)
    lane = lax.broadcasted_iota(I32, logits.shape, 1).astype(F32)
    logits = jnp.where(lane < n_experts, logits, -jnp.inf)
    v1 = jnp.max(logits, axis=1, keepdims=True)
    i1 = jnp.min(jnp.where(logits == v1, lane, float(LANES)), axis=1, keepdims=True)
    rest = jnp.where(lane == i1, -jnp.inf, logits)
    v2 = jnp.max(rest, axis=1, keepdims=True)
    i2 = jnp.min(jnp.where(rest == v2, lane, float(LANES)), axis=1, keepdims=True)
    e2 = jnp.exp(v2 - v1)
    g1 = 1.0 / (1.0 + e2)
    g2 = e2 / (1.0 + e2)
    comb_ref[...] = jnp.where(lane == 0.0, i1, jnp.where(lane == 1.0, i2, jnp.where(
        lane == 2.0, g1, jnp.where(lane == 3.0, g2, 0.0))))


def _router(xb, rw, *, n_experts):
    n, d = xb.shape
    bm = _pick(n, 1024)
    return pl.pallas_call(
        functools.partial(_router_kernel, n_experts=n_experts),
        grid=(n // bm,),
        in_specs=[pl.BlockSpec((bm, d), lambda i: (i, 0)),
                  pl.BlockSpec((d, LANES), lambda i: (0, 0))],
        out_specs=pl.BlockSpec((bm, LANES), lambda i: (i, 0)),
        out_shape=jax.ShapeDtypeStruct((n, LANES), F32),
        compiler_params=_params(("parallel",)),
        name="router",
    )(xb, rw)


def _row_gather_kernel(idx_ref, src_ref, dst_ref, sem, *, rows, window):

    def row_copy(r):
        return pltpu.make_async_copy(src_ref.at[pl.ds(idx_ref[r], 1)],
                                     dst_ref.at[pl.ds(r, 1)], sem)

    def prime(r, carry):
        row_copy(r).start()
        return carry

    def steady(r, carry):
        row_copy(r - window).wait()
        row_copy(r).start()
        return carry

    def drain(r, carry):
        row_copy(r).wait()
        return carry

    lax.fori_loop(0, window, prime, 0, unroll=8)
    lax.fori_loop(window, rows, steady, 0, unroll=8)
    lax.fori_loop(rows - window, rows, drain, 0, unroll=8)


def _row_gather(src, idx):
    n_out = idx.shape[0]
    rows = _pick(n_out, 512)
    window = 256
    assert rows >= window and rows % 8 == 0 and window % 8 == 0
    return pl.pallas_call(
        functools.partial(_row_gather_kernel, rows=rows, window=window),
        grid=(n_out // rows,),
        in_specs=[pl.BlockSpec((rows,), lambda i: (i,), memory_space=pltpu.SMEM),
                  pl.BlockSpec(memory_space=pl.ANY)],
        out_specs=pl.BlockSpec((rows,) + src.shape[1:], lambda i: (i, 0)),
        out_shape=jax.ShapeDtypeStruct((n_out,) + src.shape[1:], src.dtype),
        scratch_shapes=[pltpu.SemaphoreType.DMA(())],
        compiler_params=_params(("arbitrary",)),
        name="row_gather",
    )(idx, src)


def _moe_up_kernel(te_ref, tv_ref, x_ref, wg_ref, wu_ref, o_ref):
    t = pl.program_id(1)

    @pl.when(tv_ref[t] != 0)
    def _():
        o_ref[...] = _swiglu(x_ref[...].astype(BF16), wg_ref[...], wu_ref[...]).astype(o_ref.dtype)

    @pl.when(tv_ref[t] == 0)
    def _():
        o_ref[...] = jnp.zeros(o_ref.shape, o_ref.dtype)


def _moe_up(xs, wg, wu, tile_expert, tile_valid, *, bm):
    n_rows, d = xs.shape
    f = wg.shape[2]
    bn = f // 4 if (f // 4) % LANES == 0 else _pick(f, 1024)
    grid_spec = pltpu.PrefetchScalarGridSpec(
        num_scalar_prefetch=2,
        grid=(f // bn, n_rows // bm),
        in_specs=[pl.BlockSpec((bm, d), lambda j, t, te, tv: (t, 0)),
                  pl.BlockSpec((None, d, bn), lambda j, t, te, tv: (te[t], 0, j)),
                  pl.BlockSpec((None, d, bn), lambda j, t, te, tv: (te[t], 0, j))],
        out_specs=pl.BlockSpec((bm, bn), lambda j, t, te, tv: (t, j)))
    return pl.pallas_call(
        _moe_up_kernel,
        grid_spec=grid_spec,
        out_shape=jax.ShapeDtypeStruct((n_rows, f), BF16),
        compiler_params=_params(("arbitrary", "arbitrary")),
        name="moe_up",
    )(tile_expert, tile_valid, xs, wg, wu)


def _moe_down_kernel(te_ref, tv_ref, h_ref, w_ref, o_ref):
    t = pl.program_id(0)

    @pl.when(pl.program_id(1) == 0)
    def _():
        o_ref[...] = jnp.zeros(o_ref.shape, o_ref.dtype)

    @pl.when(tv_ref[t] != 0)
    def _():
        o_ref[...] += jnp.dot(h_ref[...], w_ref[...], preferred_element_type=F32)


def _moe_down(hs, wd, tile_expert, tile_valid, *, bm):
    n_rows, f = hs.shape
    d = wd.shape[2]
    bk = f // 2 if (f // 2) % LANES == 0 else f
    grid_spec = pltpu.PrefetchScalarGridSpec(
        num_scalar_prefetch=2,
        grid=(n_rows // bm, f // bk),
        in_specs=[pl.BlockSpec((bm, bk), lambda t, k, te, tv: (t, k)),
                  pl.BlockSpec((None, bk, d), lambda t, k, te, tv: (te[t], k, 0))],
        out_specs=pl.BlockSpec((bm, d), lambda t, k, te, tv: (t, 0)))
    return pl.pallas_call(
        _moe_down_kernel,
        grid_spec=grid_spec,
        out_shape=jax.ShapeDtypeStruct((n_rows, d), F32),
        compiler_params=_params(("arbitrary", "arbitrary")),
        name="moe_down",
    )(tile_expert, tile_valid, hs, wd)


def _moe_combine_kernel(y1_ref, y2_ref, r_ref, x_ref, g_ref, b_ref, of_ref, ob_ref, *, alpha):
    route = r_ref[...]
    y = route[:, 2:3] * y1_ref[...] + route[:, 3:4] * y2_ref[...]
    out = _layer_norm(alpha * x_ref[...] + y, g_ref[...], b_ref[...])
    of_ref[...] = out
    ob_ref[...] = out.astype(BF16)


def _moe_combine(y_both, route, x, g, b, *, alpha):
    n, d = x.shape
    bm = _pick(n, 256)
    nb = n // bm
    return pl.pallas_call(
        functools.partial(_moe_combine_kernel, alpha=alpha),
        grid=(nb,),
        in_specs=[pl.BlockSpec((bm, d), lambda i: (i, 0)),
                  pl.BlockSpec((bm, d), lambda i: (i + nb, 0)),
                  pl.BlockSpec((bm, LANES), lambda i: (i, 0)),
                  pl.BlockSpec((bm, d), lambda i: (i, 0)),
                  pl.BlockSpec((1, d), lambda i: (0, 0)),
                  pl.BlockSpec((1, d), lambda i: (0, 0))],
        out_specs=[pl.BlockSpec((bm, d), lambda i: (i, 0)),
                   pl.BlockSpec((bm, d), lambda i: (i, 0))],
        out_shape=[jax.ShapeDtypeStruct((n, d), F32), jax.ShapeDtypeStruct((n, d), BF16)],
        compiler_params=_params(("parallel",)),
        name="moe_combine",
    )(y_both, y_both, route, x, g.reshape(1, d), b.reshape(1, d))


def _route_plan(route, n_experts, bm):
    n = route.shape[0]
    n_assign = TOP_K_EXPERTS * n
    n_rows = n_assign + n_experts * bm
    flat_e = route[:, :TOP_K_EXPERTS].astype(I32).reshape(n_assign)
    onehot = (flat_e[:, None] == jnp.arange(n_experts, dtype=I32)[None, :]).astype(I32)
    csum = jnp.cumsum(onehot, axis=0)
    rank = jnp.sum(onehot * csum, axis=1) - 1
    counts = csum[-1]
    padded = (counts + bm - 1) // bm * bm
    ends = jnp.cumsum(padded)
    starts = ends - padded
    dest = jnp.sum(onehot * starts[None, :], axis=1) + rank
    src = jnp.zeros((n_rows,), I32).at[dest].set(jnp.arange(n_assign, dtype=I32) // TOP_K_EXPERTS)
    tile_start = jnp.arange(n_rows // bm, dtype=I32) * bm
    tile_expert = jnp.minimum(jnp.sum((tile_start[:, None] >= ends[None, :]).astype(I32), axis=1),
                              n_experts - 1)
    tile_valid = (tile_start < ends[-1]).astype(I32)
    return src, dest, tile_expert, tile_valid


def _rope_tables(positions):
    pos = positions.astype(F32)[..., None]
    inv_h = ROPE_THETA ** (-jnp.arange(0, HEAD_DIM, 2, dtype=F32) / HEAD_DIM)
    inv_i = ROPE_THETA ** (-jnp.arange(0, IDX_DIM, 2, dtype=F32) / IDX_DIM)
    ang_h = pos * inv_h
    ang_i = pos * inv_i
    ch, sh = jnp.cos(ang_h), jnp.sin(ang_h)
    ci, si = jnp.cos(ang_i), jnp.sin(ang_i)
    z32 = jnp.zeros_like(si)
    one64 = jnp.ones(ci.shape[:-1] + (IDX_DIM,), F32)
    z64 = jnp.zeros_like(one64)
    cat = lambda *a: jnp.concatenate(a, axis=-1).reshape(-1, LANES)
    main = (cat(ch, ch), cat(-sh, sh), cat(ci, ci, ci, ci), cat(-si, z32, -si, z32),
            cat(z32, si, z32, si))
    small = (main[0], main[1], cat(ci, ci, one64), cat(-si, z32, z64), cat(z32, si, z64))
    return main, small


def kernel(x, positions, ln_in_g, ln_in_b, w_in, w_proj_a, w_proj_b, w_out, ln_mix_g, ln_mix_b,
           ffn_w_gate, ffn_w_up, ffn_w_down, router_w, moe_w_gate, moe_w_up, moe_w_down,
           ln_ffn_g, ln_ffn_b):
    bsz, s_len, d = x.shape
    depth = w_in.shape[0]
    n = bsz * s_len
    n_heads = d // (2 * HEAD_DIM)
    d_a = n_heads * HEAD_DIM
    d_qi = IDX_HEADS * IDX_DIM
    n_experts = router_w.shape[-1]
    alpha = (2.0 * depth) ** 0.25
    n_sel = min(TOPK_MAX, s_len // 4)
    tq = 256
    assert s_len % tq == 0 and tq % CHUNK == 0 and n_sel <= tq and d_qi % d_a == 0

    off_ki = 3 * d_a + d_qi
    off_qb = off_ki + IDX_DIM + IDX_HEADS
    qb_col_block = (3 * d_a + d_qi) // d_a
    ga_col = 6 * d_a + d_qi

    tabs_main, tabs_small = _rope_tables(positions)
    tri = (lax.broadcasted_iota(I32, (tq, tq), 0) >= lax.broadcasted_iota(I32, (tq, tq), 1)
           ).astype(BF16)

    xf, xb = _input_ln(x.reshape(n, d), ln_in_g, ln_in_b)
    for layer in range(depth):
        w_l = w_in[layer]
        w_main = jnp.concatenate([w_l[:, :off_ki], w_l[:, off_qb:]], axis=1).astype(BF16)
        w_small = jnp.pad(w_l[:, off_ki:off_qb],
                          ((0, 0), (0, LANES - IDX_DIM - IDX_HEADS))).astype(BF16)
        hmain = _inproj(xb, w_main, tabs_main, bn=d_a, rope128_tiles=2,
                        rope64_tiles=(3, 3 + d_qi // d_a), stick_q_tile=qb_col_block,
                        out_dtype=BF16)
        hsmall = _inproj(xb, w_small, tabs_small, bn=LANES, rope128_tiles=0,
                         rope64_tiles=(0, 1), stick_q_tile=-1, out_dtype=F32)
        h3 = hmain.reshape(bsz, s_len, -1)
        hs3 = hsmall.reshape(bsz, s_len, LANES)

        qat = h3[..., :d_a].reshape(bsz, s_len, n_heads, HEAD_DIM).transpose(0, 2, 3, 1)
        vat = (h3[..., 2 * d_a:3 * d_a].reshape(bsz, s_len // tq, tq, n_heads, HEAD_DIM)
               .transpose(0, 3, 1, 4, 2))
        vat = jnp.concatenate(
            [vat, jnp.ones((bsz, n_heads, s_len // tq, ONES_ROWS, tq), BF16)], axis=3)
        qit = (h3[..., 3 * d_a:3 * d_a + d_qi].reshape(bsz, s_len, IDX_HEADS, IDX_DIM)
               .transpose(0, 2, 3, 1))
        ki = hs3[..., :IDX_DIM].astype(BF16)
        wit = hs3[..., IDX_DIM:IDX_DIM + IDX_HEADS].transpose(0, 2, 1)

        bias = _indexer(qit, ki, wit, tri, tq=tq, n_sel=n_sel)
        oat = _sparse_attn(qat, h3, vat, bias, tq=tq, n_heads=n_heads, k_col_block=1)
        o_a = oat.transpose(0, 3, 1, 2).reshape(n, d_a)
        o_b = _stick_attn(h3, tri, tq=tq, n_heads=n_heads, q_col_block=qb_col_block
                          ).reshape(n, d_a)

        merged = _gated_proj(o_a, o_b, w_proj_a[layer].astype(BF16), w_proj_b[layer].astype(BF16),
                             hmain, ga_col=ga_col)
        xf, xb = _mm_res_ln(merged, w_out[layer].astype(BF16), xf, ln_mix_g[layer],
                            ln_mix_b[layer], alpha=alpha, bm=_pick(n, 512), bk=d)

        i = layer // 2
        if layer % 2 == 0:
            hid = _swiglu_hidden(xb, ffn_w_gate[i].astype(BF16), ffn_w_up[i].astype(BF16))
            f_dim = hid.shape[1]
            bk = f_dim // 2 if (f_dim // 2) % LANES == 0 else f_dim
            xf, xb = _mm_res_ln(hid, ffn_w_down[i].astype(BF16), xf, ln_ffn_g[layer],
                                ln_ffn_b[layer], alpha=alpha, bm=_pick(n, 512), bk=bk)
        else:
            rw = jnp.pad(router_w[i], ((0, 0), (0, LANES - n_experts))).astype(BF16)
            route = _router(xb, rw, n_experts=n_experts)
            bm_e = _pick(n, 512)
            src, dest, tile_expert, tile_valid = _route_plan(route, n_experts, bm_e)
            xs = _row_gather(xf, src)
            hs = _moe_up(xs, moe_w_gate[i].astype(BF16), moe_w_up[i].astype(BF16),
                         tile_expert, tile_valid, bm=bm_e)
            ys = _moe_down(hs, moe_w_down[i].astype(BF16), tile_expert, tile_valid, bm=bm_e)
            y_both = _row_gather(ys, dest.reshape(n, TOP_K_EXPERTS).T.reshape(-1))
            xf, xb = _moe_combine(y_both, route, xf, ln_ffn_g[layer], ln_ffn_b[layer],
                                  alpha=alpha)
    return xf.reshape(bsz, s_len, d)
```
